```python
import math, functools
import jax, jax.numpy as jnp
from jax import lax
import numpy as np

D_MODEL = 1024
BATCH = 8
SEQ = 2048
DEPTH = 4
DEC_BATCH = 128
DEC_SEQ = 8
PAST_LEN = 2048
PAGE_SIZE = 128

CHUNK = 128
A_GROUPS = 4
A_WIDTH = D_MODEL // 4
A_GDIM = A_WIDTH // A_GROUPS
GLA_HEADS = 4
GLA_DK = D_MODEL // 32
GLA_DV = D_MODEL // 16
GLA_RANK = 16
GLA_TAU = 16.0
GLA_BLOCK = 16
DIFF_HEADS = 4
DIFF_D = D_MODEL // 16
DIFF_DV = 2 * DIFF_D
Q_BLOCK = 128
PEER_HEADS = 8
PEER_KEYS = 128
PEER_EXPERTS = PEER_KEYS * PEER_KEYS
PEER_DKEY = 128
PEER_TOPK = 16
PEER_TOK_BLOCK = 256
EPS = 1e-6

SPLITS = (A_WIDTH, A_WIDTH,
          GLA_HEADS * GLA_DK, GLA_HEADS * GLA_DK, GLA_HEADS * GLA_DV, GLA_HEADS * GLA_DV, GLA_RANK,
          DIFF_HEADS * 2 * DIFF_D, DIFF_HEADS * 2 * DIFF_D, DIFF_HEADS * DIFF_DV,
          D_MODEL, D_MODEL, D_MODEL)
IN_COLS = 2 * A_WIDTH + 2 * GLA_HEADS * GLA_DK + 2 * GLA_HEADS * GLA_DV + GLA_RANK + DIFF_HEADS * (4 * DIFF_D + DIFF_DV) + 3 * D_MODEL

kernel_name = "hybrid_gated_gmlp_gla_diffattn_peer_adaln_step"


def rmsnorm(x, g):
    xf = x.astype(jnp.float32)
    y = xf * lax.rsqrt(jnp.mean(jnp.square(xf), axis=-1, keepdims=True) + EPS)
    return (y * g.astype(jnp.float32)).astype(x.dtype)


def layernorm(x, g):
    xf = x.astype(jnp.float32)
    xc = xf - jnp.mean(xf, axis=-1, keepdims=True)
    y = xc * lax.rsqrt(jnp.mean(jnp.square(xc), axis=-1, keepdims=True) + EPS)
    return (y * g.astype(jnp.float32)).astype(x.dtype)


def ada_modulation(c, w, b):
    m = jax.nn.silu(c) @ w + b
    return jnp.split(m[:, None, :], 6, axis=-1)


def chunk_mlp_mixer(u, v_n, w_sp, b_sp):
    bsz, L, _ = u.shape
    c = CHUNK if L >= CHUNK else L
    n = L // c
    vg = v_n.reshape(bsz, n, c, A_GROUPS, A_GDIM)
    causal = jnp.tril(jnp.ones((c, c), dtype=bool))
    w = jnp.where(causal[None], w_sp[:, :c, :c], jnp.zeros((), w_sp.dtype))
    mixed = jnp.einsum('gts,bnsgd->bntgd', w, vg) + jnp.transpose(b_sp[:, :c])[None, None, :, :, None]
    return u * mixed.reshape(bsz, L, A_WIDTH)


def gla_mixer(q, k, v, log_a, s0):
    bsz, L, H, DK = q.shape
    DV = v.shape[-1]
    c = math.gcd(L, GLA_BLOCK)
    n = L // c
    f32 = jnp.float32

    def blocks(t):
        return jnp.transpose(t.astype(f32).reshape(bsz, n, c, H, t.shape[-1]), (1, 0, 3, 2, 4))

    qb, kb, vb, lb = blocks(q), blocks(k), blocks(v), blocks(log_a)
    cum = jnp.cumsum(lb, axis=-2)
    qd = qb * jnp.exp(cum)
    kd = kb * jnp.exp(-cum)
    kend = kb * jnp.exp(cum[..., -1:, :] - cum)
    att = jnp.einsum('nbhtk,nbhsk->nbhts', qd, kd)
    att = jnp.where(jnp.tril(jnp.ones((c, c), dtype=bool)), att, 0.0)
    intra = jnp.einsum('nbhts,nbhsv->nbhtv', att, vb)

    def step(s, inp):
        qd_i, kend_i, v_i, tot_i = inp
        o_inter = jnp.einsum('bhtk,bhkv->bhtv', qd_i, s)
        s_new = s * jnp.exp(tot_i)[..., None] + jnp.einsum('bhsk,bhsv->bhkv', kend_i, v_i)
        return s_new, o_inter

    s_fin, inter = lax.scan(step, s0.astype(f32), (qd, kend, vb, cum[..., -1, :]))
    o = jnp.transpose(intra + inter, (1, 0, 3, 2, 4)).reshape(bsz, L, H, DV)
    return o.astype(v.dtype), s_fin.astype(s0.dtype)


def diff_softmax_pair(q, k, v, mask, lam):
    q1, q2 = jnp.split(q, 2, axis=-1)
    k1, k2 = jnp.split(k, 2, axis=-1)
    scale = DIFF_D ** -0.5
    s1 = jnp.einsum('bthd,bshd->bhts', q1, k1).astype(jnp.float32) * scale
    s2 = jnp.einsum('bthd,bshd->bhts', q2, k2).astype(jnp.float32) * scale
    s1 = jnp.where(mask, s1, -jnp.inf)
    s2 = jnp.where(mask, s2, -jnp.inf)
    a = jax.nn.softmax(s1, axis=-1) - lam * jax.nn.softmax(s2, axis=-1)
    return jnp.einsum('bhts,bshv->bthv', a.astype(v.dtype), v)


def diff_attend_prompt(q, k, v, lam):
    bsz, L, H, dq = q.shape
    nb = L // Q_BLOCK
    qb = jnp.moveaxis(q.reshape(bsz, nb, Q_BLOCK, H, dq), 1, 0)
    kpos = jnp.arange(L)

    def one_block(args):
        qi, bi = args
        qpos = bi * Q_BLOCK + jnp.arange(Q_BLOCK)
        return diff_softmax_pair(qi, k, v, kpos[None, :] <= qpos[:, None], lam)

    o = lax.map(one_block, (qb, jnp.arange(nb)))
    return jnp.moveaxis(o, 0, 1).reshape(bsz, L, H, v.shape[-1])


def diff_attend_sample(q, k, v, lam, k_past, v_past):
    T = q.shape[1]
    P = k_past.shape[1]
    k_all = jnp.concatenate([k_past, k], axis=1)
    v_all = jnp.concatenate([v_past, v], axis=1)
    kpos = jnp.arange(P + T)
    qpos = P + jnp.arange(T)
    return diff_softmax_pair(q, k_all, v_all, kpos[None, :] <= qpos[:, None], lam)


def peer_ffn(x, w_pq, sub_k1, sub_k2, peer_u, peer_v):
    bsz, L, D = x.shape
    xt = x.reshape(-1, D)
    n = xt.shape[0]
    blk = math.gcd(n, PEER_TOK_BLOCK)
    half = PEER_DKEY // 2

    def one_block(xb):
        q = (xb @ w_pq).reshape(blk, PEER_HEADS, 2, half)
        s1 = jnp.einsum('thd,hkd->thk', q[:, :, 0], sub_k1).astype(jnp.float32)
        s2 = jnp.einsum('thd,hkd->thk', q[:, :, 1], sub_k2).astype(jnp.float32)
        v1, i1 = lax.top_k(s1, PEER_TOPK)
        v2, i2 = lax.top_k(s2, PEER_TOPK)
        cand = (v1[..., :, None] + v2[..., None, :]).reshape(blk, PEER_HEADS, PEER_TOPK * PEER_TOPK)
        cidx = (i1[..., :, None] * PEER_KEYS + i2[..., None, :]).reshape(blk, PEER_HEADS, PEER_TOPK * PEER_TOPK)
        sv, si = lax.top_k(cand, PEER_TOPK)
        eidx = jnp.take_along_axis(cidx, si, axis=-1)
        g = jax.nn.softmax(sv, axis=-1)
        u = peer_u[eidx]
        act = jax.nn.gelu(jnp.einsum('td,thkd->thk', xb, u).astype(jnp.float32))
        wts = (g * act).astype(xb.dtype)
        return jnp.einsum('thk,thkd->td', wts, peer_v[eidx])

    out = lax.map(one_block, xt.reshape(n // blk, blk, D))
    return out.reshape(bsz, L, D)


def trunk_layer(x, c, lw, lam_init, gla_s0, attend):
    bsz, L, _ = x.shape
    sh1, sc1, ga1, sh2, sc2, ga2 = ada_modulation(c, lw['w_ada'], lw['b_ada'])
    h = rmsnorm(x, lw['norm1_g']) * (1 + sc1) + sh1
    points = [int(p) for p in np.cumsum(SPLITS)[:-1]]
    (a_u, a_v, b_q, b_k, b_v, b_r, b_a, c_q, c_k, c_v,
     g_a, g_b, g_c) = jnp.split(h @ lw['w_in'], points, axis=-1)

    a_vn = layernorm(a_v, lw['a_norm_g'])
    y_a = chunk_mlp_mixer(a_u, a_vn, lw['w_sp'], lw['b_sp']) @ lw['w_oa']

    log_a = jax.nn.log_sigmoid((b_a @ lw['w_alpha'] + lw['b_alpha']).astype(jnp.float32)) / GLA_TAU
    o_b, s_b = gla_mixer(b_q.reshape(bsz, L, GLA_HEADS, GLA_DK) * (GLA_DK ** -0.5),
                         b_k.reshape(bsz, L, GLA_HEADS, GLA_DK),
                         b_v.reshape(bsz, L, GLA_HEADS, GLA_DV),
                         log_a.reshape(bsz, L, GLA_HEADS, GLA_DK), gla_s0)
    o_b = rmsnorm(o_b, lw['gla_norm_g']).reshape(bsz, L, GLA_HEADS * GLA_DV) * jax.nn.silu(b_r)
    y_b = o_b @ lw['w_ob']

    lam = (jnp.exp(jnp.sum(lw['lam_q1'] * lw['lam_k1']).astype(jnp.float32))
           - jnp.exp(jnp.sum(lw['lam_q2'] * lw['lam_k2']).astype(jnp.float32)) + lam_init)
    kc = c_k.reshape(bsz, L, DIFF_HEADS, 2 * DIFF_D)
    vc = c_v.reshape(bsz, L, DIFF_HEADS, DIFF_DV)
    o_c = attend(c_q.reshape(bsz, L, DIFF_HEADS, 2 * DIFF_D), kc, vc, lam)
    o_c = rmsnorm(o_c, lw['diff_norm_g']) * (1.0 - lam_init)
    y_c = o_c.reshape(bsz, L, DIFF_HEADS * DIFF_DV) @ lw['w_oc']

    merged = jax.nn.sigmoid(g_a) * y_a + jax.nn.sigmoid(g_b) * y_b + jax.nn.sigmoid(g_c) * y_c
    x = x + ga1 * (merged @ lw['w_out'])

    h2 = rmsnorm(x, lw['norm2_g']) * (1 + sc2) + sh2
    x = x + ga2 * peer_ffn(h2, lw['w_pq'], lw['sub_k1'], lw['sub_k2'], lw['peer_u'], lw['peer_v'])
    return x, a_vn, s_b, kc, vc


def setup_inputs(seed: int = 0) -> dict:
    key = jax.random.key(seed)
    keys = iter(jax.random.split(key, 40))
    f32 = jnp.float32

    def normal(shape, scale):
        return jax.random.normal(next(keys), shape, f32) * scale

    def gain(shape):
        return 1.0 + normal(shape, 0.02)

    n_pages = PAST_LEN // PAGE_SIZE
    n_used = DEC_BATCH * n_pages
    n_pool = n_used + (n_used + 3) // 4
    page_table = jax.random.permutation(next(keys), n_pool)[:n_used].reshape(DEC_BATCH, n_pages).astype(jnp.int32)
    D = D_MODEL
    return {
        'x_prompt': normal((BATCH, SEQ, D), 1.0),
        'x_sample': normal((DEC_BATCH, DEC_SEQ, D), 1.0),
        'cache_k': normal((DEPTH, n_pool, PAGE_SIZE, DIFF_HEADS, 2 * DIFF_D), 1.0),
        'cache_v': normal((DEPTH, n_pool, PAGE_SIZE, DIFF_HEADS, DIFF_DV), 1.0),
        'state_gla': normal((DEPTH, DEC_BATCH, GLA_HEADS, GLA_DK, GLA_DV), 0.3),
        'page_table': page_table,
        'c_prompt': normal((BATCH, D), 1.0),
        'c_sample': normal((DEC_BATCH, D), 1.0),
        'norm1_g': gain((DEPTH, D)),
        'norm2_g': gain((DEPTH, D)),
        'w_ada': normal((DEPTH, D, 6 * D), 0.5 * D ** -0.5),
        'b_ada': normal((DEPTH, 6 * D), 0.02),
        'w_in': normal((DEPTH, D, IN_COLS), D ** -0.5),
        'a_norm_g': gain((DEPTH, A_WIDTH)),
        'w_sp': normal((DEPTH, A_GROUPS, CHUNK, CHUNK), CHUNK ** -0.5),
        'b_sp': gain((DEPTH, A_GROUPS, CHUNK)),
        'w_oa': normal((DEPTH, A_WIDTH, D), A_WIDTH ** -0.5),
        'w_alpha': normal((DEPTH, GLA_RANK, GLA_HEADS * GLA_DK), GLA_RANK ** -0.5),
        'b_alpha': normal((DEPTH, GLA_HEADS * GLA_DK), 0.02),
        'gla_norm_g': gain((DEPTH, GLA_DV)),
        'w_ob': normal((DEPTH, GLA_HEADS * GLA_DV, D), (GLA_HEADS * GLA_DV) ** -0.5),
        'lam_q1': normal((DEPTH, DIFF_D), 0.1),
        'lam_k1': normal((DEPTH, DIFF_D), 0.1),
        'lam_q2': normal((DEPTH, DIFF_D), 0.1),
        'lam_k2': normal((DEPTH, DIFF_D), 0.1),
        'diff_norm_g': gain((DEPTH, DIFF_DV)),
        'w_oc': normal((DEPTH, DIFF_HEADS * DIFF_DV, D), (DIFF_HEADS * DIFF_DV) ** -0.5),
        'w_out': normal((DEPTH, D, D), D ** -0.5),
        'w_pq': normal((DEPTH, D, PEER_HEADS * PEER_DKEY), D ** -0.5),
        'sub_k1': normal((DEPTH, PEER_HEADS, PEER_KEYS, PEER_DKEY // 2), (PEER_DKEY // 2) ** -0.5),
        'sub_k2': normal((DEPTH, PEER_HEADS, PEER_KEYS, PEER_DKEY // 2), (PEER_DKEY // 2) ** -0.5),
        'peer_u': normal((DEPTH, PEER_EXPERTS, D), D ** -0.5),
        'peer_v': normal((DEPTH, PEER_EXPERTS, D), PEER_HEADS ** -0.5),
        'final_g': gain((D,)),
    }


def reference(x_prompt, x_sample, cache_k, cache_v, state_gla, page_table, c_prompt, c_sample,
              norm1_g, norm2_g, w_ada, b_ada, w_in, a_norm_g, w_sp, b_sp, w_oa, w_alpha, b_alpha,
              gla_norm_g, w_ob, lam_q1, lam_k1, lam_q2, lam_k2, diff_norm_g, w_oc, w_out,
              w_pq, sub_k1, sub_k2, peer_u, peer_v, final_g):
    dec_batch, n_pages = page_table.shape
    past_len = n_pages * cache_k.shape[2]
    xp, xs = x_prompt, x_sample
    kp_l, vp_l, sp_l, ks_l, vs_l, ss_l, as_l = [], [], [], [], [], [], []
    for l in range(DEPTH):
        lw = {'w_ada': w_ada[l], 'b_ada': b_ada[l], 'norm1_g': norm1_g[l], 'norm2_g': norm2_g[l],
              'w_in': w_in[l], 'a_norm_g': a_norm_g[l], 'w_sp': w_sp[l], 'b_sp': b_sp[l], 'w_oa': w_oa[l],
              'w_alpha': w_alpha[l], 'b_alpha': b_alpha[l], 'gla_norm_g': gla_norm_g[l], 'w_ob': w_ob[l],
              'lam_q1': lam_q1[l], 'lam_k1': lam_k1[l], 'lam_q2': lam_q2[l], 'lam_k2': lam_k2[l],
              'diff_norm_g': diff_norm_g[l], 'w_oc': w_oc[l], 'w_out': w_out[l], 'w_pq': w_pq[l],
              'sub_k1': sub_k1[l], 'sub_k2': sub_k2[l], 'peer_u': peer_u[l], 'peer_v': peer_v[l]}
        lam_init = 0.8 - 0.6 * math.exp(-0.3 * l)

        s0_p = jnp.zeros((xp.shape[0],) + state_gla.shape[2:], state_gla.dtype)
        xp, _, s_p, k_p, v_p = trunk_layer(xp, c_prompt, lw, lam_init, s0_p, diff_attend_prompt)

        k_past = cache_k[l, page_table].reshape(dec_batch, past_len, DIFF_HEADS, 2 * DIFF_D)
        v_past = cache_v[l, page_table].reshape(dec_batch, past_len, DIFF_HEADS, DIFF_DV)
        attend_s = functools.partial(diff_attend_sample, k_past=k_past, v_past=v_past)
        xs, a_s, s_s, k_s, v_s = trunk_layer(xs, c_sample, lw, lam_init, state_gla[l], attend_s)

        kp_l.append(k_p); vp_l.append(v_p); sp_l.append(s_p)
        ks_l.append(k_s); vs_l.append(v_s); ss_l.append(s_s); as_l.append(a_s)

    y_prompt = rmsnorm(xp, final_g)
    y_sample = rmsnorm(xs, final_g)
    new_k_prompt = jnp.stack(kp_l)
    new_v_prompt = jnp.stack(vp_l)
    new_state_gla_prompt = jnp.stack(sp_l)
    new_k_sample = jnp.stack(ks_l)
    new_v_sample = jnp.stack(vs_l)
    new_state_gla_sample = jnp.stack(ss_l)
    new_chunk_v_sample = jnp.stack(as_l)
    return (y_prompt, y_sample, new_k_prompt, new_v_prompt, new_state_gla_prompt,
            new_k_sample, new_v_sample, new_state_gla_sample, new_chunk_v_sample)
```

```python
import functools
import math

import jax
import jax.numpy as jnp
from jax import lax
from jax.experimental import pallas as pl
from jax.experimental.pallas import tpu as pltpu

F32 = jnp.float32
MXU_DTYPE = jnp.bfloat16
EPS = 1e-6

LANES, SUBLANES = 128, 8
CHUNK = 128
A_GROUPS, A_WIDTH = 4, 256
GLA_HEADS, GLA_DK, GLA_DV = 4, 32, 64
GLA_TAU, GLA_BLOCK = 16.0, 16
DIFF_HEADS, DIFF_D, DIFF_DV = 4, 64, 128
PEER_HEADS, PEER_KEYS, PEER_TOPK = 8, 128, 16
VMEM_LIMIT = 56 * 1024 * 1024


def _mm(a, b):
    return jnp.dot(a.astype(MXU_DTYPE), b.astype(MXU_DTYPE), preferred_element_type=F32)


def _mm_nt(a, b):
    return lax.dot_general(a.astype(MXU_DTYPE), b.astype(MXU_DTYPE), (((1,), (1,)), ((), ())),
                           preferred_element_type=F32)


def _mm_f32(a, b):
    return jnp.dot(a, b, precision=lax.Precision.HIGHEST, preferred_element_type=F32)


def _mm_nt_f32(a, b):
    return lax.dot_general(a, b, (((1,), (1,)), ((), ())), precision=lax.Precision.HIGHEST,
                           preferred_element_type=F32)


def _iota(shape, dim):
    return lax.broadcasted_iota(jnp.int32, shape, dim)


def _mod(ref):
    return ref[0] if len(ref.shape) == 3 else ref[...]


def _rms_mod(x, g, sc, sh):
    y = x * lax.rsqrt(jnp.mean(x * x, axis=-1, keepdims=True) + EPS)
    return (y * g) * (1.0 + sc) + sh


def _params(sem, vmem=VMEM_LIMIT):
    return pltpu.CompilerParams(dimension_semantics=sem, vmem_limit_bytes=vmem)


def _ada_kernel(c_ref, w_ref, b_ref, o_ref):
    c = c_ref[...]
    o_ref[0] = _mm(c * jax.nn.sigmoid(c), w_ref[0]) + b_ref[0]


def _ada(c_all, w_ada, b_ada):
    depth, d, d6 = w_ada.shape
    n = c_all.shape[0]
    tn = 1024
    return pl.pallas_call(
        _ada_kernel,
        grid=(depth, d6 // tn),
        in_specs=[pl.BlockSpec((n, d), lambda l, j: (0, 0)),
                  pl.BlockSpec((1, d, tn), lambda l, j: (l, 0, j)),
                  pl.BlockSpec((1, 1, tn), lambda l, j: (l, 0, j))],
        out_specs=pl.BlockSpec((1, n, tn), lambda l, j: (l, 0, j)),
        out_shape=jax.ShapeDtypeStruct((depth, n, d6), F32),
        compiler_params=_params(("parallel", "parallel")),
        name="ada_modulation",
    )(c_all, w_ada, b_ada.reshape(depth, 1, d6))


def _in_kernel(c, x_ref, sh_ref, sc_ref, g1_ref, wa_ref, wb_ref, wc_ref, wal_ref, bal_ref, ang_ref, wsp_ref,
               bsp_ref, amix_ref, avn_ref, bq_ref, bk_ref, bv_ref, br_ref, la_ref, cq_ref, ck_ref, cv_ref):
    tm = x_ref.shape[0]
    h = _rms_mod(x_ref[...], g1_ref[...], _mod(sc_ref), _mod(sh_ref)).astype(MXU_DTYPE)

    pc = jnp.dot(h, wc_ref[...], preferred_element_type=F32)
    cq_ref[...] = pc[:, 0:512]
    ck_ref[...] = pc[:, 512:1024]
    cv_ref[...] = pc[:, 1024:1536]

    pb = jnp.dot(h, wb_ref[...], preferred_element_type=F32)
    bq_ref[...] = pb[:, 0:128]
    bk_ref[...] = pb[:, 128:256]
    bv_ref[...] = pb[:, 256:512]
    br_ref[...] = pb[:, 512:768]
    z = _mm(pb[:, 768:896], wal_ref[...]) + bal_ref[...]
    la_ref[...] = -(jnp.maximum(-z, 0.0) + jnp.log1p(jnp.exp(-jnp.abs(z)))) * (1.0 / GLA_TAU)

    pa = jnp.dot(h, wa_ref[...], preferred_element_type=F32)
    a_u, a_v = pa[:, 0:A_WIDTH], pa[:, A_WIDTH:2 * A_WIDTH]
    vc = a_v - jnp.mean(a_v, axis=-1, keepdims=True)
    vn = vc * lax.rsqrt(jnp.mean(vc * vc, axis=-1, keepdims=True) + EPS) * ang_ref[...]
    avn_ref[...] = vn

    lc = int(math.log2(c))
    row, col = _iota((CHUNK, CHUNK), 0), _iota((CHUNK, CHUNK), 1)
    keep = ((row >> lc) == (col >> lc)) & ((col & (c - 1)) <= (row & (c - 1)))
    mst = jnp.concatenate([jnp.where(keep, wsp_ref[g], 0.0) for g in range(A_GROUPS)], axis=0).astype(MXU_DTYPE)
    lane_group = _iota((CHUNK, A_WIDTH), 1) >> 6
    bias = bsp_ref[...]
    for r in range(tm // CHUNK):
        rows = slice(r * CHUNK, (r + 1) * CHUNK)
        res = jnp.dot(mst, vn[rows].astype(MXU_DTYPE), preferred_element_type=F32)
        mixed = bias
        for g in range(A_GROUPS):
            mixed = mixed + jnp.where(lane_group == g, res[g * CHUNK:(g + 1) * CHUNK], 0.0)
        amix_ref[rows, :] = a_u[rows] * mixed


def _in_proj(x, mod, mod_spec, c, g1, wa, wb, wc, wal, bal, ang, wsp_tile, bsp_full):
    n, d = x.shape
    tm = min(512, n)
    full = lambda a: pl.BlockSpec(a.shape, lambda i: (0,) * a.ndim)
    widths = (256, 256, 128, 128, 256, 256, 128, 512, 512, 512)
    return pl.pallas_call(
        functools.partial(_in_kernel, c),
        grid=(n // tm,),
        in_specs=[pl.BlockSpec((tm, d), lambda i: (i, 0)), mod_spec(tm, 0), mod_spec(tm, 1), full(g1), full(wa),
                  full(wb), full(wc), full(wal), full(bal), full(ang), full(wsp_tile), full(bsp_full)],
        out_specs=[pl.BlockSpec((tm, w), lambda i: (i, 0)) for w in widths],
        out_shape=[jax.ShapeDtypeStruct((n, w), F32) for w in widths],
        compiler_params=_params(("parallel",)),
        name="in_proj_mixer_a",
    )(x, mod, mod, g1, wa, wb, wc, wal, bal, ang, wsp_tile, bsp_full)


def _gla_kernel(c, carry, q_ref, k_ref, v_ref, r_ref, la_ref, s0_ref, g_ref, o_ref, sout_ref, stk_ref, *st_ref):
    nb = CHUNK // c
    lc = int(math.log2(c))
    dv_all, dk_all = GLA_HEADS * GLA_DV, GLA_HEADS * GLA_DK
    la = la_ref[...]
    row, col = _iota((CHUNK, CHUNK), 0), _iota((CHUNK, CHUNK), 1)
    same = (row >> lc) == (col >> lc)
    causal = same & (col <= row)
    cum = _mm_f32(jnp.where(causal, 1.0, 0.0), la)
    tot = _mm_f32(jnp.where(same, 1.0, 0.0), la)
    k, v = k_ref[...], v_ref[...]
    qd = q_ref[...] * (GLA_DK ** -0.5) * jnp.exp(cum)
    kd = k * jnp.exp(-cum)
    kend = k * jnp.exp(tot - cum)
    dec = jnp.exp(tot)

    lane_head = _iota((CHUNK, dk_all), 1) >> 5
    qh = jnp.concatenate([jnp.where(lane_head == h, qd, 0.0) for h in range(GLA_HEADS)], axis=0)
    att = _mm_nt(qh, kd)
    att = jnp.where(jnp.concatenate([causal] * GLA_HEADS, axis=0), att, 0.0)
    res = _mm(att, v)
    v_head = _iota((CHUNK, dv_all), 1) >> 6
    o = jnp.zeros((CHUNK, dv_all), F32)
    for h in range(GLA_HEADS):
        o = o + jnp.where(v_head == h, res[h * CHUNK:(h + 1) * CHUNK], 0.0)

    vt = v.T
    t_blk = _iota((dv_all, CHUNK), 1) >> lc
    vte = jnp.concatenate([jnp.where(t_blk == n, vt, 0.0) for n in range(nb)], axis=0)
    kvt = _mm(vte, kend)
    bd = (_iota((dv_all, dk_all), 0) >> 6) == (_iota((dv_all, dk_all), 1) >> 5)

    if carry:
        st = st_ref[0]

        @pl.when(pl.program_id(1) == 0)
        def _():
            st[...] = s0_ref[0]

        s = st[...]
        for n in range(nb):
            stk_ref[n * dv_all:(n + 1) * dv_all, :] = s
            s = s * dec[n * c:n * c + 1, :] + jnp.where(bd, kvt[n * dv_all:(n + 1) * dv_all], 0.0)
        st[...] = s

        @pl.when(pl.program_id(1) == pl.num_programs(1) - 1)
        def _():
            sout_ref[0] = s
    else:
        for n in range(nb):
            s = s0_ref[n]
            stk_ref[n * dv_all:(n + 1) * dv_all, :] = s
            sout_ref[n] = s * dec[n * c:n * c + 1, :] + jnp.where(bd, kvt[n * dv_all:(n + 1) * dv_all], 0.0)

    inter = _mm_nt(qd, stk_ref[...])
    r_blk = _iota((CHUNK, dv_all), 0) >> lc
    for n in range(nb):
        o = o + jnp.where(r_blk == n, inter[:, n * dv_all:(n + 1) * dv_all], 0.0)

    gi, gj = _iota((dv_all, dv_all), 0) >> 6, _iota((dv_all, dv_all), 1) >> 6
    ms = _mm_f32(o * o, jnp.where(gi == gj, 1.0 / GLA_DV, 0.0))
    r = r_ref[...]
    o_ref[...] = o * lax.rsqrt(ms + EPS) * g_ref[...] * (r * jax.nn.sigmoid(r))


def _gla(bq, bk, bv, br, la, s0, g_tiled, seq_len):
    n = bq.shape[0]
    dv_all, dk_all = GLA_HEADS * GLA_DV, GLA_HEADS * GLA_DK
    carry = seq_len >= CHUNK
    c = GLA_BLOCK if carry else math.gcd(seq_len, GLA_BLOCK)
    nb = CHUNK // c
    if carry:
        cps = seq_len // CHUNK
        grid = (n // seq_len, cps)
        tok = lambda w: pl.BlockSpec((CHUNK, w), lambda b, ci: (b * cps + ci, 0))
        st_spec = pl.BlockSpec((1, dv_all, dk_all), lambda b, ci: (b, 0, 0))
        g_spec = pl.BlockSpec((1, dv_all), lambda b, ci: (0, 0))
        scratch = [pltpu.VMEM((nb * dv_all, dk_all), F32), pltpu.VMEM((dv_all, dk_all), F32)]
        sem = ("parallel", "arbitrary")
    else:
        assert seq_len == c and n % CHUNK == 0
        grid = (n // CHUNK,)
        tok = lambda w: pl.BlockSpec((CHUNK, w), lambda i: (i, 0))
        st_spec = pl.BlockSpec((nb, dv_all, dk_all), lambda i: (i, 0, 0))
        g_spec = pl.BlockSpec((1, dv_all), lambda i: (0, 0))
        scratch = [pltpu.VMEM((nb * dv_all, dk_all), F32)]
        sem = ("parallel",)
    return pl.pallas_call(
        functools.partial(_gla_kernel, c, carry),
        grid=grid,
        in_specs=[tok(dk_all), tok(dk_all), tok(dv_all), tok(dv_all), tok(dk_all), st_spec, g_spec],
        out_specs=[tok(dv_all), st_spec],
        out_shape=[jax.ShapeDtypeStruct((n, dv_all), F32), jax.ShapeDtypeStruct(s0.shape, F32)],
        scratch_shapes=scratch,
        compiler_params=_params(sem),
        name="gla_mixer",
    )(bq, bk, bv, br, la, s0, g_tiled)


def _state_to_bd(s):
    eye = jnp.eye(GLA_HEADS, dtype=s.dtype)
    st = jnp.transpose(s, (0, 1, 3, 2))
    return jnp.einsum('bhvk,hg->bhvgk', st, eye).reshape(s.shape[0], GLA_HEADS * GLA_DV, GLA_HEADS * GLA_DK)


def _state_from_bd(sb):
    b = sb.shape[0]
    s5 = sb.reshape(b, GLA_HEADS, GLA_DV, GLA_HEADS, GLA_DK)
    idx = jnp.arange(GLA_HEADS)
    diag = s5[:, idx, :, idx, :]
    return jnp.transpose(diag, (1, 0, 3, 2))


def _lam(lamp_ref, lam_init):
    p = lamp_ref[...]
    a = jnp.sum(p[0:1] * p[1:2], axis=-1, keepdims=True)
    b = jnp.sum(p[2:3] * p[3:4], axis=-1, keepdims=True)
    return jnp.exp(a) - jnp.exp(b) + lam_init


def _attn_prompt_kernel(tq, lam_init, q_ref, k_ref, v_ref, lamp_ref, g_ref, o_ref):
    qi = pl.program_id(2)
    lane = _iota((tq, 2 * DIFF_D), 1)
    q = q_ref[...] * (DIFF_D ** -0.5)
    q1 = jnp.where(lane < DIFF_D, q, 0.0).astype(MXU_DTYPE)
    q2 = jnp.where(lane >= DIFF_D, q, 0.0).astype(MXU_DTYPE)

    def step(j, carry, masked):
        start = pl.multiple_of(j * tq, tq)
        kj = k_ref[pl.ds(start, tq), :].astype(MXU_DTYPE)
        vj = v_ref[pl.ds(start, tq), :].astype(MXU_DTYPE)
        out = []
        for qm, (m, l, acc) in zip((q1, q2), carry):
            s = lax.dot_general(qm, kj, (((1,), (1,)), ((), ())), preferred_element_type=F32)
            if masked:
                s = jnp.where(_iota((tq, tq), 1) <= _iota((tq, tq), 0), s, -jnp.inf)
            m_new = jnp.maximum(m, jnp.max(s, axis=-1, keepdims=True))
            alpha = jnp.exp(m - m_new)
            p = jnp.exp(s - m_new)
            l = alpha * l + jnp.sum(p, axis=-1, keepdims=True)
            acc = alpha * acc + jnp.dot(p.astype(MXU_DTYPE), vj, preferred_element_type=F32)
            out.append((m_new, l, acc))
        return tuple(out)

    init = tuple((jnp.full((tq, 1), -jnp.inf, F32), jnp.zeros((tq, 1), F32), jnp.zeros((tq, DIFF_DV), F32))
                 for _ in range(2))
    carry = lax.fori_loop(0, qi, lambda j, cr: step(j, cr, False), init)
    (_, l1, acc1), (_, l2, acc2) = step(qi, carry, True)
    o = acc1 / l1 - _lam(lamp_ref, lam_init) * (acc2 / l2)
    o = o * lax.rsqrt(jnp.mean(o * o, axis=-1, keepdims=True) + EPS) * g_ref[...]
    o_ref[...] = o * (1.0 - lam_init)


def _attn_prompt(cq, ck, cv, lamp, g, seq_len, lam_init):
    n = cq.shape[0]
    tq = 256
    nq = seq_len // tq
    hd = 2 * DIFF_D
    return pl.pallas_call(
        functools.partial(_attn_prompt_kernel, tq, lam_init),
        grid=(n // seq_len, DIFF_HEADS, nq),
        in_specs=[pl.BlockSpec((tq, hd), lambda b, h, qi: (b * nq + qi, h)),
                  pl.BlockSpec((seq_len, hd), lambda b, h, qi: (b, h)),
                  pl.BlockSpec((seq_len, DIFF_DV), lambda b, h, qi: (b, h)),
                  pl.BlockSpec(lamp.shape, lambda b, h, qi: (0, 0)),
                  pl.BlockSpec(g.shape, lambda b, h, qi: (0, 0))],
        out_specs=pl.BlockSpec((tq, DIFF_DV), lambda b, h, qi: (b * nq + qi, h)),
        out_shape=jax.ShapeDtypeStruct((n, DIFF_HEADS * DIFF_DV), F32),
        compiler_params=_params(("parallel", "parallel", "parallel")),
        name="diff_attn_prompt",
    )(cq, ck, cv, lamp, g)


def _attn_sample_kernel(n_pages, lam_init, pt_ref, q_ref, kn_ref, vn_ref, lamp_ref, g_ref, *refs):
    k_pages, v_pages, o_ref = refs[:n_pages], refs[n_pages:2 * n_pages], refs[2 * n_pages]
    t_new = q_ref.shape[0]
    hd_all = DIFF_HEADS * 2 * DIFF_D
    page = k_pages[0].shape[1]
    q = q_ref[...] * (DIFF_D ** -0.5)
    lane = _iota((t_new, hd_all), 1)
    qbd = jnp.concatenate(
        [jnp.where((lane >> 6) == 2 * h + m, q, 0.0) for m in range(2) for h in range(DIFF_HEADS)],
        axis=0).astype(MXU_DTYPE)
    nrow = qbd.shape[0]
    pad = jnp.zeros((page - t_new, hd_all), F32)
    k_new = jnp.concatenate([kn_ref[...], pad], axis=0)
    v_new = jnp.concatenate([vn_ref[...], pad], axis=0)
    rt = _iota((nrow, page), 0) & (t_new - 1)
    cj = _iota((nrow, page), 1)
    s_new = jnp.where((cj < t_new) & (cj <= rt), _mm_nt(qbd, k_new), -jnp.inf)
    scores = [_mm_nt(qbd, kp[0]) for kp in k_pages] + [s_new]
    m = functools.reduce(jnp.maximum, [jnp.max(s, axis=-1, keepdims=True) for s in scores])
    l = jnp.zeros((nrow, 1), F32)
    acc = jnp.zeros((nrow, hd_all), F32)
    for s, vp in zip(scores, [vr[0] for vr in v_pages] + [v_new]):
        p = jnp.exp(s - m)
        l = l + jnp.sum(p, axis=-1, keepdims=True)
        acc = acc + _mm(p, vp)
    acc = acc / l
    half = nrow // 2
    od = acc[0:half] - _lam(lamp_ref, lam_init) * acc[half:nrow]
    lane_head = _iota((t_new, hd_all), 1) >> 7
    o = jnp.zeros((t_new, hd_all), F32)
    for h in range(DIFF_HEADS):
        o = o + jnp.where(lane_head == h, od[h * t_new:(h + 1) * t_new], 0.0)
    g = g_ref[...]
    for h in range(DIFF_HEADS):
        oh = o[:, h * DIFF_DV:(h + 1) * DIFF_DV]
        oh = oh * lax.rsqrt(jnp.mean(oh * oh, axis=-1, keepdims=True) + EPS) * g
        o_ref[:, h * DIFF_DV:(h + 1) * DIFF_DV] = oh * (1.0 - lam_init)


def _attn_sample(cq, ck, cv, cache_k, cache_v, page_table, layer, lamp, g, t_new, lam_init):
    n = cq.shape[0]
    nbatch, n_pages = page_table.shape
    depth, n_pool, page = cache_k.shape[:3]
    hd_all = DIFF_HEADS * 2 * DIFF_D
    kc = cache_k.reshape(depth * n_pool, page, hd_all)
    vc = cache_v.reshape(depth * n_pool, page, DIFF_HEADS * DIFF_DV)
    base = layer * n_pool
    tok = pl.BlockSpec((t_new, hd_all), lambda b, pt: (b, 0))
    page_spec = lambda p: pl.BlockSpec((1, page, hd_all), lambda b, pt: (base + pt[b, p], 0, 0))
    grid_spec = pltpu.PrefetchScalarGridSpec(
        num_scalar_prefetch=1,
        grid=(nbatch,),
        in_specs=[tok, tok, tok, pl.BlockSpec(lamp.shape, lambda b, pt: (0, 0)),
                  pl.BlockSpec(g.shape, lambda b, pt: (0, 0))]
                 + [page_spec(p) for p in range(n_pages)] * 2,
        out_specs=tok)
    return pl.pallas_call(
        functools.partial(_attn_sample_kernel, n_pages, lam_init),
        grid_spec=grid_spec,
        out_shape=jax.ShapeDtypeStruct((n, hd_all), F32),
        compiler_params=_params(("parallel",)),
        name="diff_attn_sample",
    )(page_table, cq, ck, cv, lamp, g, *([kc] * n_pages), *([vc] * n_pages))


def _merge_kernel(x_ref, amix_ref, ob_ref, oc_ref, sh_ref, sc_ref, ga_ref, g1_ref, woa_ref, wob_ref, woc_ref,
                  wg_ref, wout_ref, x1_ref):
    d = x_ref.shape[1]
    x = x_ref[...]
    h = _rms_mod(x, g1_ref[...], _mod(sc_ref), _mod(sh_ref)).astype(MXU_DTYPE)
    gates = jnp.dot(h, wg_ref[...], preferred_element_type=F32)
    merged = jax.nn.sigmoid(gates[:, 0:d]) * _mm(amix_ref[...], woa_ref[...])
    merged = merged + jax.nn.sigmoid(gates[:, d:2 * d]) * _mm(ob_ref[...], wob_ref[...])
    merged = merged + jax.nn.sigmoid(gates[:, 2 * d:3 * d]) * _mm(oc_ref[...], woc_ref[...])
    x1_ref[...] = x + _mod(ga_ref) * _mm(merged, wout_ref[...])


def _merge(x, amix, ob, oc, mod, mod_spec, g1, woa, wob, woc, wg, wout):
    n, d = x.shape
    tm = min(256, n)
    full = lambda a: pl.BlockSpec(a.shape, lambda i: (0,) * a.ndim)
    tok = lambda a: pl.BlockSpec((tm, a.shape[1]), lambda i: (i, 0))
    return pl.pallas_call(
        _merge_kernel,
        grid=(n // tm,),
        in_specs=[tok(x), tok(amix), tok(ob), tok(oc), mod_spec(tm, 0), mod_spec(tm, 1), mod_spec(tm, 2),
                  full(g1), full(woa), full(wob), full(woc), full(wg), full(wout)],
        out_specs=tok(x),
        out_shape=jax.ShapeDtypeStruct((n, d), F32),
        compiler_params=_params(("parallel",)),
        name="merge_out_proj",
    )(x, amix, ob, oc, mod, mod, mod, g1, woa, wob, woc, wg, wout)


_CAND_ROWS = PEER_TOPK + 7 * 8 + 8


def _peer_select_kernel(x_ref, sh_ref, sc_ref, g2_ref, wpq_ref, k1_ref, k2_ref,
                        h2_ref, rk2_ref, p2_ref, n1_ref, e1_ref, sc_scr, rk_scr, vl_scr):
    tm = x_ref.shape[0]
    nh = tm // LANES
    lnh = int(math.log2(nh))
    h2 = _rms_mod(x_ref[...], g2_ref[...], _mod(sc_ref), _mod(sh_ref))
    h2_ref[...] = h2.astype(h2_ref.dtype)
    q = _mm_f32(h2, wpq_ref[...])
    for h in range(PEER_HEADS):
        qh = q[:, h * LANES:(h + 1) * LANES]
        for side, kref in enumerate((k1_ref, k2_ref)):
            st = _mm_nt_f32(kref[h], qh)
            for hf in range(nh):
                sc_scr[2 * h + side, hf] = st[:, hf * LANES:(hf + 1) * LANES]

    key_id = _iota((PEER_KEYS, LANES), 0)
    slot_id = _iota((PEER_TOPK, LANES), 0)

    def topk(idx, _):
        i, hf = idx >> lnh, idx & (nh - 1)

        def it(kk, carry):
            s, rank, vals = carry
            m = jnp.max(s, axis=0, keepdims=True)
            first = jnp.min(jnp.where(s == m, key_id, PEER_KEYS), axis=0, keepdims=True)
            hit = key_id == first
            return (jnp.where(hit, -jnp.inf, s), jnp.where(hit, kk, rank), jnp.where(slot_id == kk, m, vals))

        _, rank, vals = lax.fori_loop(
            0, PEER_TOPK, it,
            (sc_scr[i, hf], jnp.full((PEER_KEYS, LANES), PEER_TOPK, jnp.int32), jnp.zeros((PEER_TOPK, LANES), F32)))
        rk_scr[i, hf] = rank
        vl_scr[i, hf] = vals
        return 0

    lax.fori_loop(0, 2 * PEER_HEADS * nh, topk, 0)

    r8 = _iota((8, LANES), 0)
    r16 = _iota((PEER_TOPK, LANES), 0)

    def tables(h, hf):
        v1, v2 = vl_scr[2 * h, hf], vl_scr[2 * h + 1, hf]
        rank1, rank2 = rk_scr[2 * h, hf], rk_scr[2 * h + 1, hf]
        e1s, e2s = jnp.exp(v1 - v1[0:1]), jnp.exp(v2 - v2[0:1])
        blocks = [(v1[0:1] + v2, r16, e1s[0:1] * e2s)]
        for i in range(1, 8):
            ok = r8 < PEER_TOPK // (i + 1)
            blocks.append((jnp.where(ok, v1[i:i + 1] + v2[0:8], -jnp.inf), i * PEER_TOPK + r8,
                           e1s[i:i + 1] * e2s[0:8]))
        blocks.append((v1[8:16] + v2[0:1], (r8 + 8) * PEER_TOPK, e1s[8:16] * e2s[0:1]))
        cand = jnp.concatenate([b[0] for b in blocks], axis=0)
        flat = jnp.concatenate([b[1] for b in blocks], axis=0)
        num = jnp.concatenate([b[2] for b in blocks], axis=0)

        def it(_, carry):
            cnd, taken = carry
            m = jnp.max(cnd, axis=0, keepdims=True)
            first = jnp.min(jnp.where(cnd == m, flat, PEER_TOPK * PEER_TOPK), axis=0, keepdims=True)
            hit = flat == first
            return jnp.where(hit, -jnp.inf, cnd), jnp.where(hit, 1.0, taken)

        _, taken = lax.fori_loop(0, PEER_TOPK, it, (cand, jnp.zeros((_CAND_ROWS, LANES), F32)))
        z = jnp.sum(taken * num, axis=0, keepdims=True)
        counts = [jnp.sum(taken[0:16], axis=0, keepdims=True)]
        for i in range(1, 8):
            counts.append(jnp.sum(taken[8 + 8 * i:16 + 8 * i], axis=0, keepdims=True))
        tail = taken[72:80]
        n_a = jnp.zeros((PEER_KEYS, LANES), F32)
        for i in range(PEER_TOPK):
            n_i = counts[i] if i < 8 else tail[i - 8:i - 7]
            n_a = jnp.where(rank1 == i, n_i, n_a)
        lanes = slice(hf * LANES, (hf + 1) * LANES)
        s1, s2 = sc_scr[2 * h, hf], sc_scr[2 * h + 1, hf]
        n1_ref[h, :, lanes] = n_a
        e1_ref[h, :, lanes] = jnp.where(rank1 < PEER_TOPK, jnp.exp(s1 - v1[0:1]), 0.0)
        rk2_ref[h, :, lanes] = rank2.astype(F32)
        p2_ref[h, :, lanes] = jnp.where(rank2 < PEER_TOPK, jnp.exp(s2 - v2[0:1]), 0.0) / z

    def per_head(h, _):
        for hf in range(nh):
            tables(h, hf)
        return 0

    lax.fori_loop(0, PEER_HEADS, per_head, 0)


def _peer_select(x1, mod, mod_spec, g2, wpq, k1pad, k2pad):
    n, d = x1.shape
    tm = min(256, n)
    nh = tm // LANES
    full = lambda a: pl.BlockSpec(a.shape, lambda i: (0,) * a.ndim)
    tab = pl.BlockSpec((PEER_HEADS, PEER_KEYS, tm), lambda i: (0, 0, i))
    tab_shape = jax.ShapeDtypeStruct((PEER_HEADS, PEER_KEYS, n), F32)
    return pl.pallas_call(
        _peer_select_kernel,
        grid=(n // tm,),
        in_specs=[pl.BlockSpec((tm, d), lambda i: (i, 0)), mod_spec(tm, 3), mod_spec(tm, 4), full(g2), full(wpq),
                  full(k1pad), full(k2pad)],
        out_specs=[pl.BlockSpec((tm, d), lambda i: (i, 0)), tab, tab, tab, tab],
        out_shape=[jax.ShapeDtypeStruct((n, d), MXU_DTYPE), tab_shape, tab_shape, tab_shape, tab_shape],
        scratch_shapes=[pltpu.VMEM((2 * PEER_HEADS, nh, PEER_KEYS, LANES), F32),
                        pltpu.VMEM((2 * PEER_HEADS, nh, PEER_KEYS, LANES), jnp.int32),
                        pltpu.VMEM((2 * PEER_HEADS, nh, PEER_TOPK, LANES), F32)],
        compiler_params=_params(("parallel",)),
        name="peer_select",
    )(x1, mod, mod, g2, wpq, k1pad, k2pad)


def _gelu_tanh(x):
    return x * (0.5 * (1.0 + jnp.tanh(math.sqrt(2.0 / math.pi) * (x + 0.044715 * (x * x * x)))))


def _peer_dense_kernel(h2_ref, rk2_ref, p2_ref, n1_ref, e1_ref, u_ref, vt_ref, x1_ref, ga_ref, o_ref,
                       acc_ref, st_ref, wt_ref):
    j = pl.program_id(1)
    te, tt = st_ref.shape

    @pl.when(j == 0)
    def _():
        acc_ref[...] = jnp.zeros_like(acc_ref)

    st_ref[...] = lax.dot_general(u_ref[...], h2_ref[...], (((1,), (1,)), ((), ())), preferred_element_type=F32)
    a8 = pl.ds(pl.multiple_of(j * SUBLANES, SUBLANES), SUBLANES)
    for tc in range(tt // LANES):
        lanes = slice(tc * LANES, (tc + 1) * LANES)
        for r in range(SUBLANES):
            rows = slice(r * PEER_KEYS, (r + 1) * PEER_KEYS)
            coef = jnp.zeros((PEER_KEYS, LANES), F32)
            for h in range(PEER_HEADS):
                keep = rk2_ref[h, :, lanes] < n1_ref[h, a8, lanes][r:r + 1]
                coef = coef + jnp.where(keep, p2_ref[h, :, lanes], 0.0) * e1_ref[h, a8, lanes][r:r + 1]
            wt_ref[rows, lanes] = (coef * _gelu_tanh(st_ref[rows, lanes])).astype(wt_ref.dtype)
    acc_ref[...] += jnp.dot(vt_ref[...], wt_ref[...], preferred_element_type=F32)

    @pl.when(j == pl.num_programs(1) - 1)
    def _():
        o_ref[...] = x1_ref[...] + _mod(ga_ref) * acc_ref[...].T


def _peer_dense(h2, tabs, u, vt, x1, mod, mod_spec):
    n, d = x1.shape
    n_exp = u.shape[0]
    tt = min(512, n)
    te = SUBLANES * PEER_KEYS
    tab =pl.BlockSpec((PEER_HEADS, PEER_KEYS, tt), lambda i, j: (0, 0, i))
    tok = lambda a: pl.BlockSpec((tt, a.shape[1]), lambda i, j: (i, 0))
    ms = mod_spec(tt, 5)
    ga_spec = pl.BlockSpec(ms.block_shape, lambda i, j, f=ms.index_map: f(i))
    return pl.pallas_call(
        _peer_dense_kernel,
        grid=(n // tt, n_exp // te),
        in_specs=[tok(h2), tab, tab, tab, tab,
                  pl.BlockSpec((te, d), lambda i, j: (j, 0)),
                  pl.BlockSpec((d, te), lambda i, j: (0, j)),
                  tok(x1), ga_spec],
        out_specs=tok(x1),
        out_shape=jax.ShapeDtypeStruct((n, d), F32),
        scratch_shapes=[pltpu.VMEM((d, tt), F32), pltpu.VMEM((te, tt), F32), pltpu.VMEM((te, tt), MXU_DTYPE)],
        compiler_params=_params(("parallel", "arbitrary")),
        name="peer_dense",
    )(h2, *tabs, u, vt, x1, mod)


def _final_kernel(x_ref, g_ref, o_ref):
    x = x_ref[...]
    o_ref[...] = x * lax.rsqrt(jnp.mean(x * x, axis=-1, keepdims=True) + EPS) * g_ref[...]


def _final_norm(x, g):
    n, d = x.shape
    tm = min(512, n)
    return pl.pallas_call(
        _final_kernel,
        grid=(n // tm,),
        in_specs=[pl.BlockSpec((tm, d), lambda i: (i, 0)), pl.BlockSpec((1, d), lambda i: (0, 0))],
        out_specs=pl.BlockSpec((tm, d), lambda i: (i, 0)),
        out_shape=jax.ShapeDtypeStruct((n, d), F32),
        compiler_params=_params(("parallel",)),
        name="final_norm",
    )(x, g)


def _shared_mod_spec(seq_len, d):
    def spec(tm, chunk):
        return pl.BlockSpec((1, 1, d), lambda i: ((i * tm) // seq_len, 0, chunk))
    return spec


def _per_token_mod_spec(d):
    def spec(tm, chunk):
        return pl.BlockSpec((tm, d), lambda i: (i, chunk))
    return spec


def kernel(x_prompt, x_sample, cache_k, cache_v, state_gla, page_table, c_prompt, c_sample, norm1_g, norm2_g, w_ada, b_ada, w_in, a_norm_g, w_sp, b_sp, w_oa, w_alpha, b_alpha, gla_norm_g, w_ob, lam_q1, lam_k1, lam_q2, lam_k2, diff_norm_g, w_oc, w_out, w_pq, sub_k1, sub_k2, peer_u, peer_v, final_g):
    depth = w_in.shape[0]
    bsz, seq, d = x_prompt.shape
    dbs, dseq, _ = x_sample.shape
    bf = lambda a: a.astype(MXU_DTYPE)

    wa = bf(w_in[:, :, 0:512])
    wb = bf(jnp.concatenate([w_in[:, :, 512:1280], jnp.pad(w_in[:, :, 1280:1296], ((0, 0), (0, 0), (0, 112)))], axis=2))
    wc = bf(w_in[:, :, 1296:2832])
    wg = bf(w_in[:, :, 2832:5904])
    wal = bf(jnp.pad(w_alpha, ((0, 0), (0, 112), (0, 0))))
    woa, wob, woc, wout = bf(w_oa), bf(w_ob), bf(w_oc), bf(w_out)
    u_b = bf(peer_u)
    vt_b = bf(jnp.transpose(peer_v, (0, 2, 1)))
    half = sub_k1.shape[-1]
    k1pad = jnp.pad(sub_k1, ((0, 0), (0, 0), (0, 0), (0, half)))
    k2pad = jnp.pad(sub_k2, ((0, 0), (0, 0), (0, 0), (half, 0)))
    lamp = jnp.pad(jnp.stack([lam_q1, lam_k1, lam_q2, lam_k2], axis=1), ((0, 0), (0, 0), (0, LANES - lam_q1.shape[-1])))
    gla_g = jnp.tile(gla_norm_g, (1, GLA_HEADS)).reshape(depth, 1, GLA_HEADS * GLA_DV)

    mod_all = _ada(jnp.concatenate([c_prompt, c_sample], axis=0), w_ada, b_ada)

    groups = []
    for name, x, seq_len in (("prompt", x_prompt, seq), ("sample", x_sample, dseq)):
        c = CHUNK if seq_len >= CHUNK else seq_len
        groups.append(dict(name=name, x=x.reshape(-1, d), seq_len=seq_len, c=c))

    outs = {g["name"]: dict(k=[], v=[], s=[], a=[]) for g in groups}
    for l in range(depth):
        lam_init = 0.8 - 0.6 * math.exp(-0.3 * l)
        row = lambda a: a[l].reshape(1, -1)
        for g in groups:
            n, seq_len, c = g["x"].shape[0], g["seq_len"], g["c"]
            if g["name"] == "prompt":
                mod = mod_all[l, :bsz].reshape(bsz, 1, 6 * d)
                mod_spec = _shared_mod_spec(seq_len, d)
                s0 = jnp.zeros((bsz, GLA_HEADS * GLA_DV, GLA_HEADS * GLA_DK), F32)
            else:
                mod = jnp.repeat(mod_all[l, bsz:], seq_len, axis=0)
                mod_spec = _per_token_mod_spec(d)
                s0 = _state_to_bd(state_gla[l])
            wsp_tile = jnp.tile(w_sp[l][:, :c, :c], (1, CHUNK // c, CHUNK // c))
            bsp_full = jnp.repeat(jnp.tile(b_sp[l][:, :c], (1, CHUNK // c)).T, A_WIDTH // A_GROUPS, axis=1)

            amix, avn, bq, bk, bv, br, la, cq, ck, cv = _in_proj(
                g["x"], mod, mod_spec, c, row(norm1_g), wa[l], wb[l], wc[l], wal[l], row(b_alpha), row(a_norm_g),
                wsp_tile, bsp_full)
            ob, s_new = _gla(bq, bk, bv, br, la, s0, gla_g[l], seq_len)
            if g["name"] == "prompt":
                oc = _attn_prompt(cq, ck, cv, lamp[l], row(diff_norm_g), seq_len, lam_init)
            else:
                oc = _attn_sample(cq, ck, cv, cache_k, cache_v, page_table, l, lamp[l], row(diff_norm_g), seq_len,
                                  lam_init)
            x1 = _merge(g["x"], amix, ob, oc, mod, mod_spec, row(norm1_g), woa[l], wob[l], woc[l], wg[l], wout[l])
            h2, *tabs = _peer_select(x1, mod, mod_spec, row(norm2_g), w_pq[l], k1pad[l], k2pad[l])
            g["x"] = _peer_dense(h2, tabs, u_b[l], vt_b[l], x1, mod, mod_spec)

            o = outs[g["name"]]
            o["k"].append(ck)
            o["v"].append(cv)
            o["s"].append(_state_from_bd(s_new))
            o["a"].append(avn)

    fg = final_g.reshape(1, d)
    y_prompt = _final_norm(groups[0]["x"], fg).reshape(bsz, seq, d)
    y_sample = _final_norm(groups[1]["x"], fg).reshape(dbs, dseq, d)
    kv_shape = lambda b, t: (depth, b, t, DIFF_HEADS, 2 * DIFF_D)
    op, os_ = outs["prompt"], outs["sample"]
    return (y_prompt, y_sample,
            jnp.stack(op["k"]).reshape(kv_shape(bsz, seq)), jnp.stack(op["v"]).reshape(kv_shape(bsz, seq)),
            jnp.stack(op["s"]),
            jnp.stack(os_["k"]).reshape(kv_shape(dbs, dseq)), jnp.stack(os_["v"]).reshape(kv_shape(dbs, dseq)),
            jnp.stack(os_["s"]),
            jnp.stack(os_["a"]).reshape(depth, dbs, dseq, A_WIDTH))
```

```python
import functools
import math

import jax
import jax.numpy as jnp
from jax import lax
from jax.experimental import pallas as pl
from jax.experimental.pallas import tpu as pltpu

F32 = jnp.float32
MXU_DTYPE = jnp.bfloat16
TAB_DTYPE = jnp.bfloat16
EPS = 1e-6

LANES, SUBLANES = 128, 8
CHUNK = 128
A_GROUPS, A_WIDTH = 4, 256
GLA_HEADS, GLA_DK, GLA_DV = 4, 32, 64
GLA_TAU, GLA_BLOCK = 16.0, 16
DIFF_HEADS, DIFF_D, DIFF_DV = 4, 64, 128
PEER_HEADS, PEER_KEYS, PEER_TOPK = 8, 128, 16
VMEM_LIMIT = 56 * 1024 * 1024


def _mm(a, b):
    return jnp.dot(a.astype(MXU_DTYPE), b.astype(MXU_DTYPE), preferred_element_type=F32)


def _mm_nt(a, b):
    return lax.dot_general(a.astype(MXU_DTYPE), b.astype(MXU_DTYPE), (((1,), (1,)), ((), ())),
                           preferred_element_type=F32)


def _mm_f32(a, b):
    return jnp.dot(a, b, precision=lax.Precision.HIGHEST, preferred_element_type=F32)


def _mm_nt_f32(a, b):
    return lax.dot_general(a, b, (((1,), (1,)), ((), ())), precision=lax.Precision.HIGHEST,
                           preferred_element_type=F32)


def _iota(shape, dim):
    return lax.broadcasted_iota(jnp.int32, shape, dim)


def _mod(ref):
    return ref[0] if len(ref.shape) == 3 else ref[...]


def _rms_mod(x, g, sc, sh):
    y = x * lax.rsqrt(jnp.mean(x * x, axis=-1, keepdims=True) + EPS)
    return (y * g) * (1.0 + sc) + sh


def _params(sem, vmem=VMEM_LIMIT, flags=None):
    return pltpu.CompilerParams(dimension_semantics=sem, vmem_limit_bytes=vmem, flags=flags)


def _ada_kernel(c_ref, w_ref, b_ref, o_ref):
    c = c_ref[...]
    o_ref[0] = _mm(c * jax.nn.sigmoid(c), w_ref[0]) + b_ref[0]


def _ada(c_all, w_ada, b_ada):
    depth, d, d6 = w_ada.shape
    n = c_all.shape[0]
    tn = 1024
    return pl.pallas_call(
        _ada_kernel,
        grid=(depth, d6 // tn),
        in_specs=[pl.BlockSpec((n, d), lambda l, j: (0, 0)),
                  pl.BlockSpec((1, d, tn), lambda l, j: (l, 0, j)),
                  pl.BlockSpec((1, 1, tn), lambda l, j: (l, 0, j))],
        out_specs=pl.BlockSpec((1, n, tn), lambda l, j: (l, 0, j)),
        out_shape=jax.ShapeDtypeStruct((depth, n, d6), F32),
        compiler_params=_params(("parallel", "parallel")),
        name="ada_modulation",
    )(c_all, w_ada, b_ada.reshape(depth, 1, d6))


def _in_kernel(c, x_ref, sh_ref, sc_ref, g1_ref, wa_ref, wb_ref, wc_ref, wal_ref, bal_ref, ang_ref, wsp_ref,
               bsp_ref, amix_ref, avn_ref, bq_ref, bk_ref, bv_ref, br_ref, la_ref, cq_ref, ck_ref, cv_ref):
    tm = x_ref.shape[0]
    h = _rms_mod(x_ref[...], g1_ref[...], _mod(sc_ref), _mod(sh_ref)).astype(MXU_DTYPE)

    pc = jnp.dot(h, wc_ref[...], preferred_element_type=F32)
    cq_ref[...] = pc[:, 0:512]
    ck_ref[...] = pc[:, 512:1024]
    cv_ref[...] = pc[:, 1024:1536]

    pb = jnp.dot(h, wb_ref[...], preferred_element_type=F32)
    bq_ref[...] = pb[:, 0:128]
    bk_ref[...] = pb[:, 128:256]
    bv_ref[...] = pb[:, 256:512]
    br_ref[...] = pb[:, 512:768]
    z = _mm(pb[:, 768:896], wal_ref[...]) + bal_ref[...]
    la_ref[...] = -(jnp.maximum(-z, 0.0) + jnp.log1p(jnp.exp(-jnp.abs(z)))) * (1.0 / GLA_TAU)

    pa = jnp.dot(h, wa_ref[...], preferred_element_type=F32)
    a_u, a_v = pa[:, 0:A_WIDTH], pa[:, A_WIDTH:2 * A_WIDTH]
    vc = a_v - jnp.mean(a_v, axis=-1, keepdims=True)
    vn = vc * lax.rsqrt(jnp.mean(vc * vc, axis=-1, keepdims=True) + EPS) * ang_ref[...]
    avn_ref[...] = vn

    lc = int(math.log2(c))
    row, col = _iota((CHUNK, CHUNK), 0), _iota((CHUNK, CHUNK), 1)
    keep = ((row >> lc) == (col >> lc)) & ((col & (c - 1)) <= (row & (c - 1)))
    mst = jnp.concatenate([jnp.where(keep, wsp_ref[g], 0.0) for g in range(A_GROUPS)], axis=0).astype(MXU_DTYPE)
    lane_group = _iota((CHUNK, A_WIDTH), 1) >> 6
    bias = bsp_ref[...]
    for r in range(tm // CHUNK):
        rows = slice(r * CHUNK, (r + 1) * CHUNK)
        res = jnp.dot(mst, vn[rows].astype(MXU_DTYPE), preferred_element_type=F32)
        mixed = bias
        for g in range(A_GROUPS):
            mixed = mixed + jnp.where(lane_group == g, res[g * CHUNK:(g + 1) * CHUNK], 0.0)
        amix_ref[rows, :] = a_u[rows] * mixed


def _in_proj(x, mod, mod_spec, c, g1, wa, wb, wc, wal, bal, ang, wsp_tile, bsp_full):
    n, d = x.shape
    tm = min(512, n)
    full = lambda a: pl.BlockSpec(a.shape, lambda i: (0,) * a.ndim)
    widths = (256, 256, 128, 128, 256, 256, 128, 512, 512, 512)
    return pl.pallas_call(
        functools.partial(_in_kernel, c),
        grid=(n // tm,),
        in_specs=[pl.BlockSpec((tm, d), lambda i: (i, 0)), mod_spec(tm, 0), mod_spec(tm, 1), full(g1), full(wa),
                  full(wb), full(wc), full(wal), full(bal), full(ang), full(wsp_tile), full(bsp_full)],
        out_specs=[pl.BlockSpec((tm, w), lambda i: (i, 0)) for w in widths],
        out_shape=[jax.ShapeDtypeStruct((n, w), F32) for w in widths],
        compiler_params=_params(("parallel",)),
        name="in_proj_mixer_a",
    )(x, mod, mod, g1, wa, wb, wc, wal, bal, ang, wsp_tile, bsp_full)


def _gla_kernel(c, carry, q_ref, k_ref, v_ref, r_ref, la_ref, s0_ref, g_ref, o_ref, sout_ref, stk_ref, *st_ref):
    nb = CHUNK // c
    lc = int(math.log2(c))
    dv_all, dk_all = GLA_HEADS * GLA_DV, GLA_HEADS * GLA_DK
    la = la_ref[...]
    row, col = _iota((CHUNK, CHUNK), 0), _iota((CHUNK, CHUNK), 1)
    same = (row >> lc) == (col >> lc)
    causal = same & (col <= row)
    cum = _mm_f32(jnp.where(causal, 1.0, 0.0), la)
    tot = _mm_f32(jnp.where(same, 1.0, 0.0), la)
    k, v = k_ref[...], v_ref[...]
    qd = q_ref[...] * (GLA_DK ** -0.5) * jnp.exp(cum)
    kd = k * jnp.exp(-cum)
    kend = k * jnp.exp(tot - cum)
    dec = jnp.exp(tot)

    lane_head = _iota((CHUNK, dk_all), 1) >> 5
    qh = jnp.concatenate([jnp.where(lane_head == h, qd, 0.0) for h in range(GLA_HEADS)], axis=0)
    att = _mm_nt(qh, kd)
    att = jnp.where(jnp.concatenate([causal] * GLA_HEADS, axis=0), att, 0.0)
    res = _mm(att, v)
    v_head = _iota((CHUNK, dv_all), 1) >> 6
    o = jnp.zeros((CHUNK, dv_all), F32)
    for h in range(GLA_HEADS):
        o = o + jnp.where(v_head == h, res[h * CHUNK:(h + 1) * CHUNK], 0.0)

    vt = v.T
    t_blk = _iota((dv_all, CHUNK), 1) >> lc
    vte = jnp.concatenate([jnp.where(t_blk == n, vt, 0.0) for n in range(nb)], axis=0)
    kvt = _mm(vte, kend)
    bd = (_iota((dv_all, dk_all), 0) >> 6) == (_iota((dv_all, dk_all), 1) >> 5)

    if carry:
        st = st_ref[0]

        @pl.when(pl.program_id(1) == 0)
        def _():
            st[...] = s0_ref[0]

        s = st[...]
        for n in range(nb):
            stk_ref[n * dv_all:(n + 1) * dv_all, :] = s
            s = s * dec[n * c:n * c + 1, :] + jnp.where(bd, kvt[n * dv_all:(n + 1) * dv_all], 0.0)
        st[...] = s

        @pl.when(pl.program_id(1) == pl.num_programs(1) - 1)
        def _():
            sout_ref[0] = s
    else:
        for n in range(nb):
            s = s0_ref[n]
            stk_ref[n * dv_all:(n + 1) * dv_all, :] = s
            sout_ref[n] = s * dec[n * c:n * c + 1, :] + jnp.where(bd, kvt[n * dv_all:(n + 1) * dv_all], 0.0)

    inter = _mm_nt(qd, stk_ref[...])
    r_blk = _iota((CHUNK, dv_all), 0) >> lc
    for n in range(nb):
        o = o + jnp.where(r_blk == n, inter[:, n * dv_all:(n + 1) * dv_all], 0.0)

    gi, gj = _iota((dv_all, dv_all), 0) >> 6, _iota((dv_all, dv_all), 1) >> 6
    ms = _mm_f32(o * o, jnp.where(gi == gj, 1.0 / GLA_DV, 0.0))
    r = r_ref[...]
    o_ref[...] = o * lax.rsqrt(ms + EPS) * g_ref[...] * (r * jax.nn.sigmoid(r))


def _gla(bq, bk, bv, br, la, s0, g_tiled, seq_len):
    n = bq.shape[0]
    dv_all, dk_all = GLA_HEADS * GLA_DV, GLA_HEADS * GLA_DK
    carry = seq_len >= CHUNK
    c = GLA_BLOCK if carry else math.gcd(seq_len, GLA_BLOCK)
    nb = CHUNK // c
    if carry:
        cps = seq_len // CHUNK
        grid = (n // seq_len, cps)
        tok = lambda w: pl.BlockSpec((CHUNK, w), lambda b, ci: (b * cps + ci, 0))
        st_spec = pl.BlockSpec((1, dv_all, dk_all), lambda b, ci: (b, 0, 0))
        g_spec = pl.BlockSpec((1, dv_all), lambda b, ci: (0, 0))
        scratch = [pltpu.VMEM((nb * dv_all, dk_all), F32), pltpu.VMEM((dv_all, dk_all), F32)]
        sem = ("parallel", "arbitrary")
    else:
        assert seq_len == c and n % CHUNK == 0
        grid = (n // CHUNK,)
        tok = lambda w: pl.BlockSpec((CHUNK, w), lambda i: (i, 0))
        st_spec = pl.BlockSpec((nb, dv_all, dk_all), lambda i: (i, 0, 0))
        g_spec = pl.BlockSpec((1, dv_all), lambda i: (0, 0))
        scratch = [pltpu.VMEM((nb * dv_all, dk_all), F32)]
        sem = ("parallel",)
    return pl.pallas_call(
        functools.partial(_gla_kernel, c, carry),
        grid=grid,
        in_specs=[tok(dk_all), tok(dk_all), tok(dv_all), tok(dv_all), tok(dk_all), st_spec, g_spec],
        out_specs=[tok(dv_all), st_spec],
        out_shape=[jax.ShapeDtypeStruct((n, dv_all), F32), jax.ShapeDtypeStruct(s0.shape, F32)],
        scratch_shapes=scratch,
        compiler_params=_params(sem),
        name="gla_mixer",
    )(bq, bk, bv, br, la, s0, g_tiled)


def _state_to_bd(s):
    eye = jnp.eye(GLA_HEADS, dtype=s.dtype)
    st = jnp.transpose(s, (0, 1, 3, 2))
    return jnp.einsum('bhvk,hg->bhvgk', st, eye).reshape(s.shape[0], GLA_HEADS * GLA_DV, GLA_HEADS * GLA_DK)


def _state_from_bd(sb):
    b = sb.shape[0]
    s5 = sb.reshape(b, GLA_HEADS, GLA_DV, GLA_HEADS, GLA_DK)
    idx = jnp.arange(GLA_HEADS)
    diag = s5[:, idx, :, idx, :]
    return jnp.transpose(diag, (1, 0, 3, 2))


def _lam(lamp_ref, lam_init):
    p = lamp_ref[...]
    a = jnp.sum(p[0:1] * p[1:2], axis=-1, keepdims=True)
    b = jnp.sum(p[2:3] * p[3:4], axis=-1, keepdims=True)
    return jnp.exp(a) - jnp.exp(b) + lam_init


def _attn_prompt_kernel(tq, lam_init, q_ref, k_ref, v_ref, lamp_ref, g_ref, o_ref):
    qi = pl.program_id(2)
    lane = _iota((tq, 2 * DIFF_D), 1)
    q = q_ref[...] * (DIFF_D ** -0.5)
    q1 = jnp.where(lane < DIFF_D, q, 0.0).astype(MXU_DTYPE)
    q2 = jnp.where(lane >= DIFF_D, q, 0.0).astype(MXU_DTYPE)

    def step(j, carry, masked):
        start = pl.multiple_of(j * tq, tq)
        kj = k_ref[pl.ds(start, tq), :].astype(MXU_DTYPE)
        vj = v_ref[pl.ds(start, tq), :].astype(MXU_DTYPE)
        out = []
        for qm, (m, l, acc) in zip((q1, q2), carry):
            s = lax.dot_general(qm, kj, (((1,), (1,)), ((), ())), preferred_element_type=F32)
            if masked:
                s = jnp.where(_iota((tq, tq), 1) <= _iota((tq, tq), 0), s, -jnp.inf)
            m_new = jnp.maximum(m, jnp.max(s, axis=-1, keepdims=True))
            alpha = jnp.exp(m - m_new)
            p = jnp.exp(s - m_new)
            l = alpha * l + jnp.sum(p, axis=-1, keepdims=True)
            acc = alpha * acc + jnp.dot(p.astype(MXU_DTYPE), vj, preferred_element_type=F32)
            out.append((m_new, l, acc))
        return tuple(out)

    init = tuple((jnp.full((tq, 1), -jnp.inf, F32), jnp.zeros((tq, 1), F32), jnp.zeros((tq, DIFF_DV), F32))
                 for _ in range(2))
    carry = lax.fori_loop(0, qi, lambda j, cr: step(j, cr, False), init)
    (_, l1, acc1), (_, l2, acc2) = step(qi, carry, True)
    o = acc1 / l1 - _lam(lamp_ref, lam_init) * (acc2 / l2)
    o = o * lax.rsqrt(jnp.mean(o * o, axis=-1, keepdims=True) + EPS) * g_ref[...]
    o_ref[...] = o * (1.0 - lam_init)


def _attn_prompt(cq, ck, cv, lamp, g, seq_len, lam_init):
    n = cq.shape[0]
    tq = 256
    nq = seq_len // tq
    hd = 2 * DIFF_D
    return pl.pallas_call(
        functools.partial(_attn_prompt_kernel, tq, lam_init),
        grid=(n // seq_len, DIFF_HEADS, nq),
        in_specs=[pl.BlockSpec((tq, hd), lambda b, h, qi: (b * nq + qi, h)),
                  pl.BlockSpec((seq_len, hd), lambda b, h, qi: (b, h)),
                  pl.BlockSpec((seq_len, DIFF_DV), lambda b, h, qi: (b, h)),
                  pl.BlockSpec(lamp.shape, lambda b, h, qi: (0, 0)),
                  pl.BlockSpec(g.shape, lambda b, h, qi: (0, 0))],
        out_specs=pl.BlockSpec((tq, DIFF_DV), lambda b, h, qi: (b * nq + qi, h)),
        out_shape=jax.ShapeDtypeStruct((n, DIFF_HEADS * DIFF_DV), F32),
        compiler_params=_params(("parallel", "parallel", "parallel")),
        name="diff_attn_prompt",
    )(cq, ck, cv, lamp, g)


def _attn_sample_kernel(n_pages, lam_init, pt_ref, q_ref, kn_ref, vn_ref, lamp_ref, g_ref, *refs):
    k_pages, v_pages, o_ref = refs[:n_pages], refs[n_pages:2 * n_pages], refs[2 * n_pages]
    t_new = q_ref.shape[0]
    hd_all = DIFF_HEADS * 2 * DIFF_D
    page = k_pages[0].shape[1]
    q = q_ref[...] * (DIFF_D ** -0.5)
    lane = _iota((t_new, hd_all), 1)
    qbd = jnp.concatenate(
        [jnp.where((lane >> 6) == 2 * h + m, q, 0.0) for m in range(2) for h in range(DIFF_HEADS)],
        axis=0).astype(MXU_DTYPE)
    nrow = qbd.shape[0]
    pad = jnp.zeros((page - t_new, hd_all), F32)
    k_new = jnp.concatenate([kn_ref[...], pad], axis=0)
    v_new = jnp.concatenate([vn_ref[...], pad], axis=0)
    rt = _iota((nrow, page), 0) & (t_new - 1)
    cj = _iota((nrow, page), 1)
    s_new = jnp.where((cj < t_new) & (cj <= rt), _mm_nt(qbd, k_new), -jnp.inf)
    heads_on_lanes = lambda ref: jnp.concatenate([ref[0, :, h, :] for h in range(DIFF_HEADS)], axis=-1)
    scores = [_mm_nt(qbd, heads_on_lanes(kp)) for kp in k_pages] + [s_new]
    m = functools.reduce(jnp.maximum, [jnp.max(s, axis=-1, keepdims=True) for s in scores])
    l = jnp.zeros((nrow, 1), F32)
    acc = jnp.zeros((nrow, hd_all), F32)
    for s, vr in zip(scores, list(v_pages) + [None]):
        p = jnp.exp(s - m)
        l = l + jnp.sum(p, axis=-1, keepdims=True)
        acc = acc + _mm(p, v_new if vr is None else heads_on_lanes(vr))
    acc = acc / l
    half = nrow // 2
    od = acc[0:half] - _lam(lamp_ref, lam_init) * acc[half:nrow]
    lane_head = _iota((t_new, hd_all), 1) >> 7
    o = jnp.zeros((t_new, hd_all), F32)
    for h in range(DIFF_HEADS):
        o = o + jnp.where(lane_head == h, od[h * t_new:(h + 1) * t_new], 0.0)
    g = g_ref[...]
    for h in range(DIFF_HEADS):
        oh = o[:, h * DIFF_DV:(h + 1) * DIFF_DV]
        oh = oh * lax.rsqrt(jnp.mean(oh * oh, axis=-1, keepdims=True) + EPS) * g
        o_ref[:, h * DIFF_DV:(h + 1) * DIFF_DV] = oh * (1.0 - lam_init)


def _attn_sample(cq, ck, cv, cache_k, cache_v, page_table, layer, lamp, g, t_new, lam_init):
    n = cq.shape[0]
    nbatch, n_pages = page_table.shape
    depth, n_pool, page = cache_k.shape[:3]
    hd_all = DIFF_HEADS * 2 * DIFF_D
    kc = cache_k.reshape((depth * n_pool,) + cache_k.shape[2:])
    vc = cache_v.reshape((depth * n_pool,) + cache_v.shape[2:])
    base = layer * n_pool
    tok = pl.BlockSpec((t_new, hd_all), lambda b, pt: (b, 0))
    page_spec = lambda p: pl.BlockSpec((1,) + kc.shape[1:], lambda b, pt: (base + pt[b, p], 0, 0, 0))
    grid_spec = pltpu.PrefetchScalarGridSpec(
        num_scalar_prefetch=1,
        grid=(nbatch,),
        in_specs=[tok, tok, tok, pl.BlockSpec(lamp.shape, lambda b, pt: (0, 0)),
                  pl.BlockSpec(g.shape, lambda b, pt: (0, 0))]
                 + [page_spec(p) for p in range(n_pages)] * 2,
        out_specs=tok)
    return pl.pallas_call(
        functools.partial(_attn_sample_kernel, n_pages, lam_init),
        grid_spec=grid_spec,
        out_shape=jax.ShapeDtypeStruct((n, hd_all), F32),
        compiler_params=_params(("parallel",)),
        name="diff_attn_sample",
    )(page_table, cq, ck, cv, lamp, g, *([kc] * n_pages), *([vc] * n_pages))


def _merge_kernel(x_ref, amix_ref, ob_ref, oc_ref, sh_ref, sc_ref, ga_ref, g1_ref, woa_ref, wob_ref, woc_ref,
                  wg_ref, wout_ref, x1_ref):
    d = x_ref.shape[1]
    x = x_ref[...]
    h = _rms_mod(x, g1_ref[...], _mod(sc_ref), _mod(sh_ref)).astype(MXU_DTYPE)
    gates = jnp.dot(h, wg_ref[...], preferred_element_type=F32)
    merged = jax.nn.sigmoid(gates[:, 0:d]) * _mm(amix_ref[...], woa_ref[...])
    merged = merged + jax.nn.sigmoid(gates[:, d:2 * d]) * _mm(ob_ref[...], wob_ref[...])
    merged = merged + jax.nn.sigmoid(gates[:, 2 * d:3 * d]) * _mm(oc_ref[...], woc_ref[...])
    x1_ref[...] = x + _mod(ga_ref) * _mm(merged, wout_ref[...])


def _merge(x, amix, ob, oc, mod, mod_spec, g1, woa, wob, woc, wg, wout):
    n, d = x.shape
    tm = min(256, n)
    full = lambda a: pl.BlockSpec(a.shape, lambda i: (0,) * a.ndim)
    tok = lambda a: pl.BlockSpec((tm, a.shape[1]), lambda i: (i, 0))
    return pl.pallas_call(
        _merge_kernel,
        grid=(n // tm,),
        in_specs=[tok(x), tok(amix), tok(ob), tok(oc), mod_spec(tm, 0), mod_spec(tm, 1), mod_spec(tm, 2),
                  full(g1), full(woa), full(wob), full(woc), full(wg), full(wout)],
        out_specs=tok(x),
        out_shape=jax.ShapeDtypeStruct((n, d), F32),
        compiler_params=_params(("parallel",)),
        name="merge_out_proj",
    )(x, amix, ob, oc, mod, mod, mod, g1, woa, wob, woc, wg, wout)


_CAND_ROWS = PEER_TOPK + 7 * 8 + 8


def _tab_pack():
    return 4 // jnp.dtype(TAB_DTYPE).itemsize


def _tab_container():
    return F32 if _tab_pack() == 1 else jnp.uint32


def _store_packed(ref, h, lanes, x):
    pack = _tab_pack()
    rows = SUBLANES * pack
    for kb in range(x.shape[0] // rows):
        tile = x[kb * rows:(kb + 1) * rows].astype(TAB_DTYPE)
        ref[h, kb * SUBLANES:(kb + 1) * SUBLANES, lanes] = tile if pack == 1 else pltpu.bitcast(tile, ref.dtype)


def _load_packed(ref, h, kb, lanes):
    tile = ref[h, kb * SUBLANES:(kb + 1) * SUBLANES, lanes]
    return tile if _tab_pack() == 1 else pltpu.bitcast(tile, TAB_DTYPE)


def _peer_select_kernel(x_ref, sh_ref, sc_ref, g2_ref, wpq_ref, k1_ref, k2_ref,
                        h2_ref, rk2_ref, p2_ref, n1_ref, e1_ref, sc_scr, rk_scr, vl_scr):
    tm = x_ref.shape[0]
    nh = tm // LANES
    lnh = int(math.log2(nh))
    h2 = _rms_mod(x_ref[...], g2_ref[...], _mod(sc_ref), _mod(sh_ref))
    h2_ref[...] = h2.astype(h2_ref.dtype)
    q = _mm_f32(h2, wpq_ref[...])
    for h in range(PEER_HEADS):
        qh = q[:, h * LANES:(h + 1) * LANES]
        for side, kref in enumerate((k1_ref, k2_ref)):
            st = _mm_nt_f32(kref[h], qh)
            for hf in range(nh):
                sc_scr[2 * h + side, hf] = st[:, hf * LANES:(hf + 1) * LANES]

    key_id = _iota((PEER_KEYS, LANES), 0)
    slot_id = _iota((PEER_TOPK, LANES), 0)

    def topk(idx, _):
        i, hf = idx >> lnh, idx & (nh - 1)

        def it(kk, carry):
            s, rank, vals = carry
            m = jnp.max(s, axis=0, keepdims=True)
            first = jnp.min(jnp.where(s == m, key_id, PEER_KEYS), axis=0, keepdims=True)
            hit = key_id == first
            return (jnp.where(hit, -jnp.inf, s), jnp.where(hit, kk, rank), jnp.where(slot_id == kk, m, vals))

        _, rank, vals = lax.fori_loop(
            0, PEER_TOPK, it,
            (sc_scr[i, hf], jnp.full((PEER_KEYS, LANES), PEER_TOPK, jnp.int32), jnp.zeros((PEER_TOPK, LANES), F32)))
        rk_scr[i, hf] = rank
        vl_scr[i, hf] = vals
        return 0

    lax.fori_loop(0, 2 * PEER_HEADS * nh, topk, 0)

    r8 = _iota((8, LANES), 0)
    r16 = _iota((PEER_TOPK, LANES), 0)

    def tables(h, hf):
        v1, v2 = vl_scr[2 * h, hf], vl_scr[2 * h + 1, hf]
        rank1, rank2 = rk_scr[2 * h, hf], rk_scr[2 * h + 1, hf]
        e1s, e2s = jnp.exp(v1 - v1[0:1]), jnp.exp(v2 - v2[0:1])
        blocks = [(v1[0:1] + v2, r16, e1s[0:1] * e2s)]
        for i in range(1, 8):
            ok = r8 < PEER_TOPK // (i + 1)
            blocks.append((jnp.where(ok, v1[i:i + 1] + v2[0:8], -jnp.inf), i * PEER_TOPK + r8,
                           e1s[i:i + 1] * e2s[0:8]))
        blocks.append((v1[8:16] + v2[0:1], (r8 + 8) * PEER_TOPK, e1s[8:16] * e2s[0:1]))
        cand = jnp.concatenate([b[0] for b in blocks], axis=0)
        flat = jnp.concatenate([b[1] for b in blocks], axis=0)
        num = jnp.concatenate([b[2] for b in blocks], axis=0)

        def it(_, carry):
            cnd, taken = carry
            m = jnp.max(cnd, axis=0, keepdims=True)
            first = jnp.min(jnp.where(cnd == m, flat, PEER_TOPK * PEER_TOPK), axis=0, keepdims=True)
            hit = flat == first
            return jnp.where(hit, -jnp.inf, cnd), jnp.where(hit, 1.0, taken)

        _, taken = lax.fori_loop(0, PEER_TOPK, it, (cand, jnp.zeros((_CAND_ROWS, LANES), F32)))
        z = jnp.sum(taken * num, axis=0, keepdims=True)
        counts = [jnp.sum(taken[0:16], axis=0, keepdims=True)]
        for i in range(1, 8):
            counts.append(jnp.sum(taken[8 + 8 * i:16 + 8 * i], axis=0, keepdims=True))
        tail = taken[72:80]
        n_a = jnp.zeros((PEER_KEYS, LANES), F32)
        for i in range(PEER_TOPK):
            n_i = counts[i] if i < 8 else tail[i - 8:i - 7]
            n_a = jnp.where(rank1 == i, n_i, n_a)
        lanes = slice(hf * LANES, (hf + 1) * LANES)
        s1, s2 = sc_scr[2 * h, hf], sc_scr[2 * h + 1, hf]
        n1_ref[h, :, lanes] = n_a
        e1_ref[h, :, lanes] = jnp.where(rank1 < PEER_TOPK, jnp.exp(s1 - v1[0:1]), 0.0)
        _store_packed(rk2_ref, h, lanes, rank2.astype(F32))
        _store_packed(p2_ref, h, lanes, jnp.where(rank2 < PEER_TOPK, jnp.exp(s2 - v2[0:1]), 0.0) / z)

    def per_head(h, _):
        for hf in range(nh):
            tables(h, hf)
        return 0

    lax.fori_loop(0, PEER_HEADS, per_head, 0)


def _peer_select(x1, mod, mod_spec, g2, wpq, k1pad, k2pad):
    n, d = x1.shape
    tm = min(256, n)
    nh = tm // LANES
    full = lambda a: pl.BlockSpec(a.shape, lambda i: (0,) * a.ndim)
    packed_rows = PEER_KEYS // _tab_pack()
    tab = lambda rows: pl.BlockSpec((PEER_HEADS, rows, tm), lambda i: (0, 0, i))
    tab_shape = lambda rows, dt: jax.ShapeDtypeStruct((PEER_HEADS, rows, n), dt)
    return pl.pallas_call(
        _peer_select_kernel,
        grid=(n // tm,),
        in_specs=[pl.BlockSpec((tm, d), lambda i: (i, 0)), mod_spec(tm, 3), mod_spec(tm, 4), full(g2), full(wpq),
                  full(k1pad), full(k2pad)],
        out_specs=[pl.BlockSpec((tm, d), lambda i: (i, 0)), tab(packed_rows), tab(packed_rows), tab(PEER_KEYS),
                   tab(PEER_KEYS)],
        out_shape=[jax.ShapeDtypeStruct((n, d), MXU_DTYPE), tab_shape(packed_rows, _tab_container()),
                   tab_shape(packed_rows, _tab_container()), tab_shape(PEER_KEYS, F32), tab_shape(PEER_KEYS, F32)],
        scratch_shapes=[pltpu.VMEM((2 * PEER_HEADS, nh, PEER_KEYS, LANES), F32),
                        pltpu.VMEM((2 * PEER_HEADS, nh, PEER_KEYS, LANES), jnp.int32),
                        pltpu.VMEM((2 * PEER_HEADS, nh, PEER_TOPK, LANES), F32)],
        compiler_params=_params(("parallel",)),
        name="peer_select",
    )(x1, mod, mod, g2, wpq, k1pad, k2pad)


def _gelu_tanh(x):
    return x * (0.5 * (1.0 + jnp.tanh(math.sqrt(2.0 / math.pi) * (x + 0.044715 * (x * x * x)))))


def _peer_dense_kernel(h2_ref, rk2_ref, p2_ref, n1_ref, e1_ref, u0_ref, un_ref, vt_ref, x1_ref, ga_ref, o_ref,
                       acc_ref, sta_ref, stb_ref, wt_ref):
    j = pl.program_id(1)
    te, tt = sta_ref.shape

    def scores_t(u_ref, rows):
        return lax.dot_general(u_ref[rows, :], h2_ref[...], (((1,), (1,)), ((), ())), preferred_element_type=F32)

    @pl.when(j == 0)
    def _():
        acc_ref[...] = jnp.zeros_like(acc_ref)
        sta_ref[...] = scores_t(u0_ref, slice(None))

    a8 = pl.ds(pl.multiple_of(j * SUBLANES, SUBLANES), SUBLANES)
    tab = TAB_DTYPE
    kb_rows = SUBLANES * _tab_pack()
    n_kb = PEER_KEYS // kb_rows
    r_group = SUBLANES // 2
    n_tc = tt // LANES

    def sweep(st_ref, nxt_ref):
        nxt_ref[...] = scores_t(un_ref, slice(None))
        for r0 in range(0, SUBLANES, r_group):
            for tc in range(n_tc):
                lanes = slice(tc * LANES, (tc + 1) * LANES)
                coef = [[None] * r_group for _ in range(n_kb)]
                for h in range(PEER_HEADS):
                    n8, e8 = n1_ref[h, a8, lanes], e1_ref[h, a8, lanes]
                    bcast = lambda t, r: jnp.broadcast_to(t[r:r + 1], (kb_rows, LANES)).astype(tab)
                    nb = [bcast(n8, r0 + ri) for ri in range(r_group)]
                    eb = [bcast(e8, r0 + ri) for ri in range(r_group)]
                    for kb in range(n_kb):
                        rk, p = _load_packed(rk2_ref, h, kb, lanes), _load_packed(p2_ref, h, kb, lanes)
                        for ri in range(r_group):
                            term = jnp.where(rk < nb[ri], p, jnp.zeros_like(p)) * eb[ri]
                            coef[kb][ri] = term if h == 0 else coef[kb][ri] + term
                for kb in range(n_kb):
                    for ri in range(r_group):
                        first = (r0 + ri) * PEER_KEYS + kb * kb_rows
                        rows = slice(first, first + kb_rows)
                        act = _gelu_tanh(st_ref[rows, lanes]).astype(tab)
                        wt_ref[rows, lanes] = (coef[kb][ri] * act).astype(wt_ref.dtype)
            done =slice(r0 * PEER_KEYS, (r0 + r_group) * PEER_KEYS)
            acc_ref[...] += jnp.dot(vt_ref[:, done], wt_ref[done, :], preferred_element_type=F32)

    @pl.when((j & 1) == 0)
    def _():
        sweep(sta_ref, stb_ref)

    @pl.when((j & 1) == 1)
    def _():
        sweep(stb_ref, sta_ref)

    @pl.when(j == pl.num_programs(1) - 1)
    def _():
        o_ref[...] = x1_ref[...] + _mod(ga_ref) * acc_ref[...].T


def _peer_dense(h2, tabs, u, vt, x1, mod, mod_spec):
    n, d = x1.shape
    n_exp = u.shape[0]
    tt = min(512, n)
    te = SUBLANES * PEER_KEYS
    tab = lambda a: pl.BlockSpec((PEER_HEADS, a.shape[1], tt), lambda i, j: (0, 0, i))
    tok = lambda a: pl.BlockSpec((tt, a.shape[1]), lambda i, j: (i, 0))
    ms = mod_spec(tt, 5)
    ga_spec = pl.BlockSpec(ms.block_shape, lambda i, j, f=ms.index_map: f(i))
    ne = n_exp // te
    return pl.pallas_call(
        _peer_dense_kernel,
        grid=(n // tt, ne),
        in_specs=[tok(h2)] + [tab(t) for t in tabs]
                 + [pl.BlockSpec((te, d), lambda i, j: (0, 0)),
                    pl.BlockSpec((te, d), lambda i, j: ((j + 1) % ne, 0)),
                    pl.BlockSpec((d, te), lambda i, j: (0, j)),
                    tok(x1), ga_spec],
        out_specs=tok(x1),
        out_shape=jax.ShapeDtypeStruct((n, d), F32),
        scratch_shapes=[pltpu.VMEM((d, tt), F32), pltpu.VMEM((te, tt), F32), pltpu.VMEM((te, tt), F32),
                        pltpu.VMEM((te, tt), MXU_DTYPE)],
        compiler_params=_params(("parallel", "arbitrary")),
        name="peer_dense",
    )(h2, *tabs, u, u, vt, x1, mod)


def _final_kernel(x_ref, g_ref, o_ref):
    x = x_ref[...]
    o_ref[...] = x * lax.rsqrt(jnp.mean(x * x, axis=-1, keepdims=True) + EPS) * g_ref[...]


def _final_norm(x, g):
    n, d = x.shape
    tm = min(512, n)
    return pl.pallas_call(
        _final_kernel,
        grid=(n // tm,),
        in_specs=[pl.BlockSpec((tm, d), lambda i: (i, 0)), pl.BlockSpec((1, d), lambda i: (0, 0))],
        out_specs=pl.BlockSpec((tm, d), lambda i: (i, 0)),
        out_shape=jax.ShapeDtypeStruct((n, d), F32),
        compiler_params=_params(("parallel",)),
        name="final_norm",
    )(x, g)


def _shared_mod_spec(seq_len, d):
    def spec(tm, chunk):
        return pl.BlockSpec((1, 1, d), lambda i: ((i * tm) // seq_len, 0, chunk))
    return spec


def _per_token_mod_spec(d):
    def spec(tm, chunk):
        return pl.BlockSpec((tm, d), lambda i: (i, chunk))
    return spec


def kernel(x_prompt, x_sample, cache_k, cache_v, state_gla, page_table, c_prompt, c_sample, norm1_g, norm2_g, w_ada, b_ada, w_in, a_norm_g, w_sp, b_sp, w_oa, w_alpha, b_alpha, gla_norm_g, w_ob, lam_q1, lam_k1, lam_q2, lam_k2, diff_norm_g, w_oc, w_out, w_pq, sub_k1, sub_k2, peer_u, peer_v, final_g):
    depth = w_in.shape[0]
    bsz, seq, d = x_prompt.shape
    dbs, dseq, _ = x_sample.shape
    bf = lambda a: a.astype(MXU_DTYPE)

    wa = bf(w_in[:, :, 0:512])
    wb = bf(jnp.concatenate([w_in[:, :, 512:1280], jnp.pad(w_in[:, :, 1280:1296], ((0, 0), (0, 0), (0, 112)))], axis=2))
    wc = bf(w_in[:, :, 1296:2832])
    wg = bf(w_in[:, :, 2832:5904])
    wal = bf(jnp.pad(w_alpha, ((0, 0), (0, 112), (0, 0))))
    woa, wob, woc, wout = bf(w_oa), bf(w_ob), bf(w_oc), bf(w_out)
    u_b = bf(peer_u)
    vt_b = bf(jnp.transpose(peer_v, (0, 2, 1)))
    half = sub_k1.shape[-1]
    k1pad = jnp.pad(sub_k1, ((0, 0), (0, 0), (0, 0), (0, half)))
    k2pad = jnp.pad(sub_k2, ((0, 0), (0, 0), (0, 0), (half, 0)))
    lamp = jnp.pad(jnp.stack([lam_q1, lam_k1, lam_q2, lam_k2], axis=1), ((0, 0), (0, 0), (0, LANES - lam_q1.shape[-1])))
    gla_g = jnp.tile(gla_norm_g, (1, GLA_HEADS)).reshape(depth, 1, GLA_HEADS * GLA_DV)

    mod_all = _ada(jnp.concatenate([c_prompt, c_sample], axis=0), w_ada, b_ada)

    groups = []
    for name, x, seq_len in (("prompt", x_prompt, seq), ("sample", x_sample, dseq)):
        c = CHUNK if seq_len >= CHUNK else seq_len
        groups.append(dict(name=name, x=x.reshape(-1, d), seq_len=seq_len, c=c))

    outs = {g["name"]: dict(k=[], v=[], s=[], a=[]) for g in groups}
    for l in range(depth):
        lam_init = 0.8 - 0.6 * math.exp(-0.3 * l)
        row = lambda a: a[l].reshape(1, -1)
        for g in groups:
            n, seq_len, c = g["x"].shape[0], g["seq_len"], g["c"]
            if g["name"] == "prompt":
                mod = mod_all[l, :bsz].reshape(bsz, 1, 6 * d)
                mod_spec = _shared_mod_spec(seq_len, d)
                s0 = jnp.zeros((bsz, GLA_HEADS * GLA_DV, GLA_HEADS * GLA_DK), F32)
            else:
                mod = jnp.repeat(mod_all[l, bsz:], seq_len, axis=0)
                mod_spec = _per_token_mod_spec(d)
                s0 = _state_to_bd(state_gla[l])
            wsp_tile = jnp.tile(w_sp[l][:, :c, :c], (1, CHUNK // c, CHUNK // c))
            bsp_full = jnp.repeat(jnp.tile(b_sp[l][:, :c], (1, CHUNK // c)).T, A_WIDTH // A_GROUPS, axis=1)

            amix, avn, bq, bk, bv, br, la, cq, ck, cv = _in_proj(
                g["x"], mod, mod_spec, c, row(norm1_g), wa[l], wb[l], wc[l], wal[l], row(b_alpha), row(a_norm_g),
                wsp_tile, bsp_full)
            ob, s_new = _gla(bq, bk, bv, br, la, s0, gla_g[l], seq_len)
            if g["name"] == "prompt":
                oc = _attn_prompt(cq, ck, cv, lamp[l], row(diff_norm_g), seq_len, lam_init)
            else:
                oc = _attn_sample(cq, ck, cv, cache_k, cache_v, page_table, l, lamp[l], row(diff_norm_g), seq_len,
                                  lam_init)
            x1 = _merge(g["x"], amix, ob, oc, mod, mod_spec, row(norm1_g), woa[l], wob[l], woc[l], wg[l], wout[l])
            h2, *tabs = _peer_select(x1, mod, mod_spec, row(norm2_g), w_pq[l], k1pad[l], k2pad[l])
            g["x"] = _peer_dense(h2, tabs, u_b[l], vt_b[l], x1, mod, mod_spec)

            o = outs[g["name"]]
            o["k"].append(ck)
            o["v"].append(cv)
            o["s"].append(_state_from_bd(s_new))
            o["a"].append(avn)

    fg = final_g.reshape(1, d)
    y_prompt = _final_norm(groups[0]["x"], fg).reshape(bsz, seq, d)
    y_sample = _final_norm(groups[1]["x"], fg).reshape(dbs, dseq, d)
    kv_shape = lambda b, t: (depth, b, t, DIFF_HEADS, 2 * DIFF_D)
    op, os_ = outs["prompt"], outs["sample"]
    return (y_prompt, y_sample,
            jnp.stack(op["k"]).reshape(kv_shape(bsz, seq)), jnp.stack(op["v"]).reshape(kv_shape(bsz, seq)),
            jnp.stack(op["s"]),
            jnp.stack(os_["k"]).reshape(kv_shape(dbs, dseq)), jnp.stack(os_["v"]).reshape(kv_shape(dbs, dseq)),
            jnp.stack(os_["s"]),
            jnp.stack(os_["a"]).reshape(depth, dbs, dseq, A_WIDTH))
```

```python
import functools
import math

import jax
import jax.numpy as jnp
from jax import lax
from jax.experimental import pallas as pl
from jax.experimental.pallas import tpu as pltpu

F32 = jnp.float32
MXU_DTYPE = jnp.bfloat16
TAB_DTYPE = jnp.bfloat16
EPS = 1e-6

LANES, SUBLANES = 128, 8
CHUNK = 128
A_GROUPS, A_WIDTH = 4, 256
GLA_HEADS, GLA_DK, GLA_DV = 4, 32, 64
GLA_TAU, GLA_BLOCK = 16.0, 16
DIFF_HEADS, DIFF_D, DIFF_DV = 4, 64, 128
PEER_HEADS, PEER_KEYS, PEER_TOPK = 8, 128, 16
VMEM_LIMIT = 56 * 1024 * 1024


def _mm(a, b):
    return jnp.dot(a.astype(MXU_DTYPE), b.astype(MXU_DTYPE), preferred_element_type=F32)


def _mm_nt(a, b):
    return lax.dot_general(a.astype(MXU_DTYPE), b.astype(MXU_DTYPE), (((1,), (1,)), ((), ())),
                           preferred_element_type=F32)


def _mm_f32(a, b):
    return jnp.dot(a, b, precision=lax.Precision.HIGHEST, preferred_element_type=F32)


def _iota(shape, dim):
    return lax.broadcasted_iota(jnp.int32, shape, dim)


def _mod(ref):
    return ref[0] if len(ref.shape) == 3 else ref[...]


def _rms_mod(x, g, sc, sh):
    y = x * lax.rsqrt(jnp.mean(x * x, axis=-1, keepdims=True) + EPS)
    return (y * g) * (1.0 + sc) + sh


def _params(sem, vmem=VMEM_LIMIT, flags=None):
    return pltpu.CompilerParams(dimension_semantics=sem, vmem_limit_bytes=vmem, flags=flags)


def _ada_kernel(c_ref, w_ref, b_ref, o_ref):
    c = c_ref[...]
    o_ref[0] = _mm(c * jax.nn.sigmoid(c), w_ref[0]) + b_ref[0]


def _ada(c_all, w_ada, b_ada):
    depth, d, d6 = w_ada.shape
    n = c_all.shape[0]
    tn = 1024
    return pl.pallas_call(
        _ada_kernel,
        grid=(depth, d6 // tn),
        in_specs=[pl.BlockSpec((n, d), lambda l, j: (0, 0)),
                  pl.BlockSpec((1, d, tn), lambda l, j: (l, 0, j)),
                  pl.BlockSpec((1, 1, tn), lambda l, j: (l, 0, j))],
        out_specs=pl.BlockSpec((1, n, tn), lambda l, j: (l, 0, j)),
        out_shape=jax.ShapeDtypeStruct((depth, n, d6), F32),
        compiler_params=_params(("parallel", "parallel")),
        name="ada_modulation",
    )(c_all, w_ada, b_ada.reshape(depth, 1, d6))


def _in_kernel(c, x_ref, sh_ref, sc_ref, g1_ref, wa_ref, wb_ref, wc_ref, wal_ref, bal_ref, ang_ref, wsp_ref,
               bsp_ref, amix_ref, avn_ref, bq_ref, bk_ref, bv_ref, br_ref, la_ref, cq_ref, ck_ref, cv_ref):
    tm = x_ref.shape[0]
    h = _rms_mod(x_ref[...], g1_ref[...], _mod(sc_ref), _mod(sh_ref)).astype(MXU_DTYPE)

    pc = jnp.dot(h, wc_ref[...], preferred_element_type=F32)
    cq_ref[...] = pc[:, 0:512]
    ck_ref[...] = pc[:, 512:1024]
    cv_ref[...] = pc[:, 1024:1536]

    pb = jnp.dot(h, wb_ref[...], preferred_element_type=F32)
    bq_ref[...] = pb[:, 0:128]
    bk_ref[...] = pb[:, 128:256]
    bv_ref[...] = pb[:, 256:512]
    br_ref[...] = pb[:, 512:768]
    z = _mm(pb[:, 768:896], wal_ref[...]) + bal_ref[...]
    la_ref[...] = -(jnp.maximum(-z, 0.0) + jnp.log1p(jnp.exp(-jnp.abs(z)))) * (1.0 / GLA_TAU)

    pa = jnp.dot(h, wa_ref[...], preferred_element_type=F32)
    a_u, a_v = pa[:, 0:A_WIDTH], pa[:, A_WIDTH:2 * A_WIDTH]
    vc = a_v - jnp.mean(a_v, axis=-1, keepdims=True)
    vn = vc * lax.rsqrt(jnp.mean(vc * vc, axis=-1, keepdims=True) + EPS) * ang_ref[...]
    avn_ref[...] = vn

    lc = int(math.log2(c))
    row, col = _iota((CHUNK, CHUNK), 0), _iota((CHUNK, CHUNK), 1)
    keep = ((row >> lc) == (col >> lc)) & ((col & (c - 1)) <= (row & (c - 1)))
    mst = jnp.concatenate([jnp.where(keep, wsp_ref[g], 0.0) for g in range(A_GROUPS)], axis=0).astype(MXU_DTYPE)
    lane_group = _iota((CHUNK, A_WIDTH), 1) >> 6
    bias = bsp_ref[...]
    for r in range(tm // CHUNK):
        rows = slice(r * CHUNK, (r + 1) * CHUNK)
        res = jnp.dot(mst, vn[rows].astype(MXU_DTYPE), preferred_element_type=F32)
        mixed = bias
        for g in range(A_GROUPS):
            mixed = mixed + jnp.where(lane_group == g, res[g * CHUNK:(g + 1) * CHUNK], 0.0)
        amix_ref[rows, :] = a_u[rows] * mixed


def _in_proj(x, mod, mod_spec, c, g1, wa, wb, wc, wal, bal, ang, wsp_tile, bsp_full):
    n, d = x.shape
    tm = min(512, n)
    full = lambda a: pl.BlockSpec(a.shape, lambda i: (0,) * a.ndim)
    widths = (256, 256, 128, 128, 256, 256, 128, 512, 512, 512)
    return pl.pallas_call(
        functools.partial(_in_kernel, c),
        grid=(n // tm,),
        in_specs=[pl.BlockSpec((tm, d), lambda i: (i, 0)), mod_spec(tm, 0), mod_spec(tm, 1), full(g1), full(wa),
                  full(wb), full(wc), full(wal), full(bal), full(ang), full(wsp_tile), full(bsp_full)],
        out_specs=[pl.BlockSpec((tm, w), lambda i: (i, 0)) for w in widths],
        out_shape=[jax.ShapeDtypeStruct((n, w), F32) for w in widths],
        compiler_params=_params(("parallel",)),
        name="in_proj_mixer_a",
    )(x, mod, mod, g1, wa, wb, wc, wal, bal, ang, wsp_tile, bsp_full)


def _gla_kernel(c, carry, q_ref, k_ref, v_ref, r_ref, la_ref, s0_ref, g_ref, o_ref, sout_ref, stk_ref, *st_ref):
    nb = CHUNK // c
    lc = int(math.log2(c))
    dv_all, dk_all = GLA_HEADS * GLA_DV, GLA_HEADS * GLA_DK
    la = la_ref[...]
    row, col = _iota((CHUNK, CHUNK), 0), _iota((CHUNK, CHUNK), 1)
    same = (row >> lc) == (col >> lc)
    causal = same & (col <= row)
    cum = _mm_f32(jnp.where(causal, 1.0, 0.0), la)
    tot = _mm_f32(jnp.where(same, 1.0, 0.0), la)
    k, v = k_ref[...], v_ref[...]
    qd = q_ref[...] * (GLA_DK ** -0.5) * jnp.exp(cum)
    kd = k * jnp.exp(-cum)
    kend = k * jnp.exp(tot - cum)
    dec = jnp.exp(tot)

    lane_head = _iota((CHUNK, dk_all), 1) >> 5
    qh = jnp.concatenate([jnp.where(lane_head == h, qd, 0.0) for h in range(GLA_HEADS)], axis=0)
    att = _mm_nt(qh, kd)
    att = jnp.where(jnp.concatenate([causal] * GLA_HEADS, axis=0), att, 0.0)
    res = _mm(att, v)
    v_head = _iota((CHUNK, dv_all), 1) >> 6
    o = jnp.zeros((CHUNK, dv_all), F32)
    for h in range(GLA_HEADS):
        o = o + jnp.where(v_head == h, res[h * CHUNK:(h + 1) * CHUNK], 0.0)

    vt = v.T
    t_blk = _iota((dv_all, CHUNK), 1) >> lc
    vte = jnp.concatenate([jnp.where(t_blk == n, vt, 0.0) for n in range(nb)], axis=0)
    kvt = _mm(vte, kend)
    bd = (_iota((dv_all, dk_all), 0) >> 6) == (_iota((dv_all, dk_all), 1) >> 5)

    if carry:
        st = st_ref[0]

        @pl.when(pl.program_id(1) == 0)
        def _():
            st[...] = s0_ref[0]

        s = st[...]
        for n in range(nb):
            stk_ref[n * dv_all:(n + 1) * dv_all, :] = s
            s = s * dec[n * c:n * c + 1, :] + jnp.where(bd, kvt[n * dv_all:(n + 1) * dv_all], 0.0)
        st[...] = s

        @pl.when(pl.program_id(1) == pl.num_programs(1) - 1)
        def _():
            sout_ref[0] = s
    else:
        for n in range(nb):
            s = s0_ref[n]
            stk_ref[n * dv_all:(n + 1) * dv_all, :] = s
            sout_ref[n] = s * dec[n * c:n * c + 1, :] + jnp.where(bd, kvt[n * dv_all:(n + 1) * dv_all], 0.0)

    inter = _mm_nt(qd, stk_ref[...])
    r_blk = _iota((CHUNK, dv_all), 0) >> lc
    for n in range(nb):
        o = o + jnp.where(r_blk == n, inter[:, n * dv_all:(n + 1) * dv_all], 0.0)

    gi, gj = _iota((dv_all, dv_all), 0) >> 6, _iota((dv_all, dv_all), 1) >> 6
    ms = _mm_f32(o * o, jnp.where(gi == gj, 1.0 / GLA_DV, 0.0))
    r = r_ref[...]
    o_ref[...] = o * lax.rsqrt(ms + EPS) * g_ref[...] * (r * jax.nn.sigmoid(r))


def _gla(bq, bk, bv, br, la, s0, g_tiled, seq_len):
    n = bq.shape[0]
    dv_all, dk_all = GLA_HEADS * GLA_DV, GLA_HEADS * GLA_DK
    carry = seq_len >= CHUNK
    c = GLA_BLOCK if carry else math.gcd(seq_len, GLA_BLOCK)
    nb = CHUNK // c
    if carry:
        cps = seq_len // CHUNK
        grid = (n // seq_len, cps)
        tok = lambda w: pl.BlockSpec((CHUNK, w), lambda b, ci: (b * cps + ci, 0))
        st_spec = pl.BlockSpec((1, dv_all, dk_all), lambda b, ci: (b, 0, 0))
        g_spec = pl.BlockSpec((1, dv_all), lambda b, ci: (0, 0))
        scratch = [pltpu.VMEM((nb * dv_all, dk_all), F32), pltpu.VMEM((dv_all, dk_all), F32)]
        sem = ("parallel", "arbitrary")
    else:
        assert seq_len == c and n % CHUNK == 0
        grid = (n // CHUNK,)
        tok = lambda w: pl.BlockSpec((CHUNK, w), lambda i: (i, 0))
        st_spec = pl.BlockSpec((nb, dv_all, dk_all), lambda i: (i, 0, 0))
        g_spec = pl.BlockSpec((1, dv_all), lambda i: (0, 0))
        scratch = [pltpu.VMEM((nb * dv_all, dk_all), F32)]
        sem = ("parallel",)
    return pl.pallas_call(
        functools.partial(_gla_kernel, c, carry),
        grid=grid,
        in_specs=[tok(dk_all), tok(dk_all), tok(dv_all), tok(dv_all), tok(dk_all), st_spec, g_spec],
        out_specs=[tok(dv_all), st_spec],
        out_shape=[jax.ShapeDtypeStruct((n, dv_all), F32), jax.ShapeDtypeStruct(s0.shape, F32)],
        scratch_shapes=scratch,
        compiler_params=_params(sem),
        name="gla_mixer",
    )(bq, bk, bv, br, la, s0, g_tiled)


def _state_to_bd(s):
    eye = jnp.eye(GLA_HEADS, dtype=s.dtype)
    st = jnp.transpose(s, (0, 1, 3, 2))
    return jnp.einsum('bhvk,hg->bhvgk', st, eye).reshape(s.shape[0], GLA_HEADS * GLA_DV, GLA_HEADS * GLA_DK)


def _state_from_bd(sb):
    b = sb.shape[0]
    s5 = sb.reshape(b, GLA_HEADS, GLA_DV, GLA_HEADS, GLA_DK)
    idx = jnp.arange(GLA_HEADS)
    diag = s5[:, idx, :, idx, :]
    return jnp.transpose(diag, (1, 0, 3, 2))


def _lam(lamp_ref, lam_init):
    p = lamp_ref[...]
    a = jnp.sum(p[0:1] * p[1:2], axis=-1, keepdims=True)
    b = jnp.sum(p[2:3] * p[3:4], axis=-1, keepdims=True)
    return jnp.exp(a) - jnp.exp(b) + lam_init


def _attn_prompt_kernel(tq, lam_init, q_ref, k_ref, v_ref, lamp_ref, g_ref, o_ref):
    qi = pl.program_id(2)
    lane = _iota((tq, 2 * DIFF_D), 1)
    q = q_ref[...] * (DIFF_D ** -0.5)
    q1 = jnp.where(lane < DIFF_D, q, 0.0).astype(MXU_DTYPE)
    q2 = jnp.where(lane >= DIFF_D, q, 0.0).astype(MXU_DTYPE)

    def step(j, carry, masked):
        start = pl.multiple_of(j * tq, tq)
        kj = k_ref[pl.ds(start, tq), :].astype(MXU_DTYPE)
        vj = v_ref[pl.ds(start, tq), :].astype(MXU_DTYPE)
        out = []
        for qm, (m, l, acc) in zip((q1, q2), carry):
            s = lax.dot_general(qm, kj, (((1,), (1,)), ((), ())), preferred_element_type=F32)
            if masked:
                s = jnp.where(_iota((tq, tq), 1) <= _iota((tq, tq), 0), s, -jnp.inf)
            m_new = jnp.maximum(m, jnp.max(s, axis=-1, keepdims=True))
            alpha = jnp.exp(m - m_new)
            p = jnp.exp(s - m_new)
            l = alpha * l + jnp.sum(p, axis=-1, keepdims=True)
            acc = alpha * acc + jnp.dot(p.astype(MXU_DTYPE), vj, preferred_element_type=F32)
            out.append((m_new, l, acc))
        return tuple(out)

    init = tuple((jnp.full((tq, 1), -jnp.inf, F32), jnp.zeros((tq, 1), F32), jnp.zeros((tq, DIFF_DV), F32))
                 for _ in range(2))
    carry = lax.fori_loop(0, qi, lambda j, cr: step(j, cr, False), init)
    (_, l1, acc1), (_, l2, acc2) = step(qi, carry, True)
    o = acc1 / l1 - _lam(lamp_ref, lam_init) * (acc2 / l2)
    o = o * lax.rsqrt(jnp.mean(o * o, axis=-1, keepdims=True) + EPS) * g_ref[...]
    o_ref[...] = o * (1.0 - lam_init)


def _attn_prompt(cq, ck, cv, lamp, g, seq_len, lam_init):
    n = cq.shape[0]
    tq = 256
    nq = seq_len // tq
    hd = 2 * DIFF_D
    return pl.pallas_call(
        functools.partial(_attn_prompt_kernel, tq, lam_init),
        grid=(n // seq_len, DIFF_HEADS, nq),
        in_specs=[pl.BlockSpec((tq, hd), lambda b, h, qi: (b * nq + qi, h)),
                  pl.BlockSpec((seq_len, hd), lambda b, h, qi: (b, h)),
                  pl.BlockSpec((seq_len, DIFF_DV), lambda b, h, qi: (b, h)),
                  pl.BlockSpec(lamp.shape, lambda b, h, qi: (0, 0)),
                  pl.BlockSpec(g.shape, lambda b, h, qi: (0, 0))],
        out_specs=pl.BlockSpec((tq, DIFF_DV), lambda b, h, qi: (b * nq + qi, h)),
        out_shape=jax.ShapeDtypeStruct((n, DIFF_HEADS * DIFF_DV), F32),
        compiler_params=_params(("parallel", "parallel", "parallel")),
        name="diff_attn_prompt",
    )(cq, ck, cv, lamp, g)


def _attn_sample_kernel(n_pages, lam_init, pt_ref, q_ref, kn_ref, vn_ref, lamp_ref, g_ref, *refs):
    k_pages, v_pages, o_ref = refs[:n_pages], refs[n_pages:2 * n_pages], refs[2 * n_pages]
    t_new = q_ref.shape[0]
    hd_all = DIFF_HEADS * 2 * DIFF_D
    page = k_pages[0].shape[1] // DIFF_HEADS
    q = q_ref[...] * (DIFF_D ** -0.5)
    lane = _iota((t_new, hd_all), 1)
    qbd = jnp.concatenate(
        [jnp.where((lane >> 6) == 2 * h + m, q, 0.0) for m in range(2) for h in range(DIFF_HEADS)],
        axis=0).astype(MXU_DTYPE)
    nrow = qbd.shape[0]
    pad = jnp.zeros((page - t_new, hd_all), F32)
    k_new = jnp.concatenate([kn_ref[...], pad], axis=0)
    v_new = jnp.concatenate([vn_ref[...], pad], axis=0)
    rt = _iota((nrow, page), 0) & (t_new - 1)
    cj = _iota((nrow, page), 1)
    s_new = jnp.where((cj < t_new) & (cj <= rt), _mm_nt(qbd, k_new), -jnp.inf)
    heads_on_lanes = lambda ref: jnp.concatenate(
        [ref[0, pl.ds(h, page, stride=DIFF_HEADS), :] for h in range(DIFF_HEADS)], axis=-1)
    scores = [_mm_nt(qbd, heads_on_lanes(kp)) for kp in k_pages] + [s_new]
    m = functools.reduce(jnp.maximum, [jnp.max(s, axis=-1, keepdims=True) for s in scores])
    l = jnp.zeros((nrow, 1), F32)
    acc = jnp.zeros((nrow, hd_all), F32)
    for s, vr in zip(scores, list(v_pages) + [None]):
        p = jnp.exp(s - m)
        l = l + jnp.sum(p, axis=-1, keepdims=True)
        acc = acc + _mm(p, v_new if vr is None else heads_on_lanes(vr))
    acc = acc / l
    half = nrow // 2
    od = acc[0:half] - _lam(lamp_ref, lam_init) * acc[half:nrow]
    lane_head = _iota((t_new, hd_all), 1) >> 7
    o = jnp.zeros((t_new, hd_all), F32)
    for h in range(DIFF_HEADS):
        o = o + jnp.where(lane_head == h, od[h * t_new:(h + 1) * t_new], 0.0)
    g = g_ref[...]
    for h in range(DIFF_HEADS):
        oh = o[:, h * DIFF_DV:(h + 1) * DIFF_DV]
        oh = oh * lax.rsqrt(jnp.mean(oh * oh, axis=-1, keepdims=True) + EPS) * g
        o_ref[:, h * DIFF_DV:(h + 1) * DIFF_DV] = oh * (1.0 - lam_init)


def _attn_sample(cq, ck, cv, cache_k, cache_v, page_table, layer, lamp, g, t_new, lam_init):
    n = cq.shape[0]
    nbatch, n_pages = page_table.shape
    depth, n_pool, page = cache_k.shape[:3]
    hd_all = DIFF_HEADS * 2 * DIFF_D
    kc = cache_k.reshape(depth * n_pool, page * DIFF_HEADS, cache_k.shape[-1])
    vc = cache_v.reshape(depth * n_pool, page * DIFF_HEADS, cache_v.shape[-1])
    base = layer * n_pool
    tok = pl.BlockSpec((t_new, hd_all), lambda b, pt: (b, 0))
    page_spec = lambda p: pl.BlockSpec((1,) + kc.shape[1:], lambda b, pt: (base + pt[b, p], 0, 0))
    grid_spec = pltpu.PrefetchScalarGridSpec(
        num_scalar_prefetch=1,
        grid=(nbatch,),
        in_specs=[tok, tok, tok, pl.BlockSpec(lamp.shape, lambda b, pt: (0, 0)),
                  pl.BlockSpec(g.shape, lambda b, pt: (0, 0))]
                 + [page_spec(p) for p in range(n_pages)] * 2,
        out_specs=tok)
    return pl.pallas_call(
        functools.partial(_attn_sample_kernel, n_pages, lam_init),
        grid_spec=grid_spec,
        out_shape=jax.ShapeDtypeStruct((n, hd_all), F32),
        compiler_params=_params(("parallel",)),
        name="diff_attn_sample",
    )(page_table, cq, ck, cv, lamp, g, *([kc] * n_pages), *([vc] * n_pages))


def _merge_kernel(x_ref, amix_ref, ob_ref, oc_ref, sh_ref, sc_ref, ga_ref, g1_ref, woa_ref, wob_ref, woc_ref,
                  wg_ref, wout_ref, x1_ref):
    d = x_ref.shape[1]
    x = x_ref[...]
    h = _rms_mod(x, g1_ref[...], _mod(sc_ref), _mod(sh_ref)).astype(MXU_DTYPE)
    gates = jnp.dot(h, wg_ref[...], preferred_element_type=F32)
    merged = jax.nn.sigmoid(gates[:, 0:d]) * _mm(amix_ref[...], woa_ref[...])
    merged = merged + jax.nn.sigmoid(gates[:, d:2 * d]) * _mm(ob_ref[...], wob_ref[...])
    merged = merged + jax.nn.sigmoid(gates[:, 2 * d:3 * d]) * _mm(oc_ref[...], woc_ref[...])
    x1_ref[...] = x + _mod(ga_ref) * _mm(merged, wout_ref[...])


def _merge(x, amix, ob, oc, mod, mod_spec, g1, woa, wob, woc, wg, wout):
    n, d = x.shape
    tm = min(256, n)
    full = lambda a: pl.BlockSpec(a.shape, lambda i: (0,) * a.ndim)
    tok = lambda a: pl.BlockSpec((tm, a.shape[1]), lambda i: (i, 0))
    return pl.pallas_call(
        _merge_kernel,
        grid=(n // tm,),
        in_specs=[tok(x), tok(amix), tok(ob), tok(oc), mod_spec(tm, 0), mod_spec(tm, 1), mod_spec(tm, 2),
                  full(g1), full(woa), full(wob), full(woc), full(wg), full(wout)],
        out_specs=tok(x),
        out_shape=jax.ShapeDtypeStruct((n, d), F32),
        compiler_params=_params(("parallel",)),
        name="merge_out_proj",
    )(x, amix, ob, oc, mod, mod, mod, g1, woa, wob, woc, wg, wout)


_CAND_ROWS = PEER_TOPK + 7 * 8 + 8


def _tab_pack():
    return 4 // jnp.dtype(TAB_DTYPE).itemsize


def _tab_container():
    return F32 if _tab_pack() == 1 else jnp.uint32


def _store_packed(ref, h, lanes, x):
    pack = _tab_pack()
    rows = SUBLANES * pack
    for kb in range(x.shape[0] // rows):
        tile = x[kb * rows:(kb + 1) * rows].astype(TAB_DTYPE)
        ref[h, kb * SUBLANES:(kb + 1) * SUBLANES, lanes] = tile if pack == 1 else pltpu.bitcast(tile, ref.dtype)


def _load_packed(ref, h, kb, lanes):
    tile = ref[h, kb * SUBLANES:(kb + 1) * SUBLANES, lanes]
    return tile if _tab_pack() == 1 else pltpu.bitcast(tile, TAB_DTYPE)


def _peer_select_kernel(x_ref, sh_ref, sc_ref, g2_ref, wpq_hi_ref, wpq_lo_ref, k1_ref, k2_ref,
                        h2_ref, rk2_ref, p2_ref, n1_ref, e1_ref, sc_scr, rk_scr, vl_scr):
    tm = x_ref.shape[0]
    nh = tm // LANES
    lnh = int(math.log2(nh))
    h2 = _rms_mod(x_ref[...], g2_ref[...], _mod(sc_ref), _mod(sh_ref))
    h2_hi = h2.astype(MXU_DTYPE)
    h2_ref[...] = h2_hi
    h2_lo = (h2 - h2_hi.astype(F32)).astype(MXU_DTYPE)
    w_hi = wpq_hi_ref[...]
    q = (jnp.dot(h2_hi, w_hi, preferred_element_type=F32) + jnp.dot(h2_lo, w_hi, preferred_element_type=F32)
         + jnp.dot(h2_hi, wpq_lo_ref[...], preferred_element_type=F32))
    q_hi = q.astype(MXU_DTYPE)
    q_lo = (q - q_hi.astype(F32)).astype(MXU_DTYPE)
    nt = lambda a, b: lax.dot_general(a, b, (((1,), (1,)), ((), ())), preferred_element_type=F32)
    for h in range(PEER_HEADS):
        cols = slice(h * LANES, (h + 1) * LANES)
        for side, kref in enumerate((k1_ref, k2_ref)):
            k_hi, k_lo = kref[0, h], kref[1, h]
            st = nt(k_hi, q_hi[:, cols]) + nt(k_lo, q_hi[:, cols]) + nt(k_hi, q_lo[:, cols])
            for hf in range(nh):
                sc_scr[2 * h + side, hf] = st[:, hf * LANES:(hf + 1) * LANES]

    key_id = _iota((PEER_KEYS, LANES), 0)
    slot_id = _iota((PEER_TOPK, LANES), 0)

    def topk_fast(idx, bad):
        h, hf = idx >> lnh, idx & (nh - 1)
        s_a, s_b = sc_scr[2 * h, hf], sc_scr[2 * h + 1, hf]

        def it(kk, carry):
            out = []
            for s, vals in carry:
                m = jnp.max(s, axis=0, keepdims=True)
                out.append((jnp.where(s == m, -jnp.inf, s), jnp.where(slot_id == kk, m, vals)))
            return tuple(out)

        zeros = jnp.zeros((PEER_TOPK, LANES), F32)
        (_, v_a), (_, v_b) = lax.fori_loop(0, PEER_TOPK, it, ((s_a, zeros), (s_b, zeros)))
        for side, (s, vals) in enumerate(((s_a, v_a), (s_b, v_b))):
            rank = jnp.zeros((PEER_KEYS, LANES), jnp.int32)
            for kk in range(PEER_TOPK):
                rank = rank + jnp.where(vals[kk:kk + 1] > s, 1, 0)
            ranked = jnp.sum(jnp.where(rank < PEER_TOPK, 1.0, 0.0), axis=0, keepdims=True)
            bad = jnp.maximum(bad, jnp.where(ranked == PEER_TOPK, 0.0, 1.0))
            rk_scr[2 * h + side, hf] = rank
            vl_scr[2 * h + side, hf] = vals
        return bad

    bad = lax.fori_loop(0, PEER_HEADS * nh, topk_fast, jnp.zeros((1, LANES), F32))

    def topk_exact(idx, _):
        i, hf = idx >> lnh, idx & (nh - 1)

        def it(kk, carry):
            s, rank, vals = carry
            m = jnp.max(s, axis=0, keepdims=True)
            first = jnp.min(jnp.where(s == m, key_id, PEER_KEYS), axis=0, keepdims=True)
            hit = key_id == first
            return (jnp.where(hit, -jnp.inf, s), jnp.where(hit, kk, rank), jnp.where(slot_id == kk, m, vals))

        _, rank, vals = lax.fori_loop(
            0, PEER_TOPK, it,
            (sc_scr[i, hf], jnp.full((PEER_KEYS, LANES), PEER_TOPK, jnp.int32), jnp.zeros((PEER_TOPK, LANES), F32)))
        rk_scr[i, hf] = rank
        vl_scr[i, hf] = vals
        return 0

    @pl.when(jnp.max(bad) > 0.0)
    def _():
        lax.fori_loop(0, 2 * PEER_HEADS * nh, topk_exact, 0)

    r8 = _iota((8, LANES), 0)
    r16 = _iota((PEER_TOPK, LANES), 0)

    def candidates(h, hf):
        v1, v2 = vl_scr[2 * h, hf], vl_scr[2 * h + 1, hf]
        e1s, e2s = jnp.exp(v1 - v1[0:1]), jnp.exp(v2 - v2[0:1])
        blocks = [(v1[0:1] + v2, r16, e1s[0:1] * e2s)]
        for i in range(1, 8):
            ok = r8 < PEER_TOPK // (i + 1)
            blocks.append((jnp.where(ok, v1[i:i + 1] + v2[0:8], -jnp.inf), i * PEER_TOPK + r8,
                           e1s[i:i + 1] * e2s[0:8]))
        blocks.append((v1[8:16] + v2[0:1], (r8 + 8) * PEER_TOPK, e1s[8:16] * e2s[0:1]))
        return tuple(jnp.concatenate([b[k] for b in blocks], axis=0) for k in range(3))

    def final_topk(cands, flat):
        def it(_, carry):
            out = []
            for cnd, taken in carry:
                m = jnp.max(cnd, axis=0, keepdims=True)
                first = jnp.min(jnp.where(cnd == m, flat, PEER_TOPK * PEER_TOPK), axis=0, keepdims=True)
                hit = flat == first
                out.append((jnp.where(hit, -jnp.inf, cnd), jnp.where(hit, 1.0, taken)))
            return tuple(out)

        init = tuple((c, jnp.zeros((_CAND_ROWS, LANES), F32)) for c in cands)
        return [taken for _, taken in lax.fori_loop(0, PEER_TOPK, it, init)]

    def tables(h, hf, taken, num):
        v1, v2 = vl_scr[2 * h, hf], vl_scr[2 * h + 1, hf]
        rank1, rank2 = rk_scr[2 * h, hf], rk_scr[2 * h + 1, hf]
        z = jnp.sum(taken * num, axis=0, keepdims=True)
        counts = [jnp.sum(taken[0:16], axis=0, keepdims=True)]
        for i in range(1, 8):
            counts.append(jnp.sum(taken[8 + 8 * i:16 + 8 * i], axis=0, keepdims=True))
        tail = taken[72:80]
        n_a = jnp.zeros((PEER_KEYS, LANES), F32)
        for i in range(PEER_TOPK):
            n_i = counts[i] if i < 8 else tail[i - 8:i - 7]
            n_a = jnp.where(rank1 == i, n_i, n_a)
        lanes = slice(hf * LANES, (hf + 1) * LANES)
        s1, s2 = sc_scr[2 * h, hf], sc_scr[2 * h + 1, hf]
        n1_ref[h, :, lanes] = n_a
        e1_ref[h, :, lanes] = jnp.where(rank1 < PEER_TOPK, jnp.exp(s1 - v1[0:1]), 0.0)
        _store_packed(rk2_ref, h, lanes, rank2.astype(F32))
        _store_packed(p2_ref, h, lanes, jnp.where(rank2 < PEER_TOPK, jnp.exp(s2 - v2[0:1]), 0.0) / z)

    def per_head(h, _):
        cands = [candidates(h, hf) for hf in range(nh)]
        taken = final_topk([c[0] for c in cands], cands[0][1])
        for hf in range(nh):
            tables(h, hf, taken[hf], cands[hf][2])
        return 0

    lax.fori_loop(0, PEER_HEADS, per_head, 0)


def _peer_select(x1, mod, mod_spec, g2, wpq_hi, wpq_lo, k1pad, k2pad):
    n, d = x1.shape
    tm = min(256, n)
    nh = tm // LANES
    full = lambda a: pl.BlockSpec(a.shape, lambda i: (0,) * a.ndim)
    packed_rows = PEER_KEYS // _tab_pack()
    tab = lambda rows: pl.BlockSpec((PEER_HEADS, rows, tm), lambda i: (0, 0, i))
    tab_shape = lambda rows, dt: jax.ShapeDtypeStruct((PEER_HEADS, rows, n), dt)
    return pl.pallas_call(
        _peer_select_kernel,
        grid=(n // tm,),
        in_specs=[pl.BlockSpec((tm, d), lambda i: (i, 0)), mod_spec(tm, 3), mod_spec(tm, 4), full(g2),
                  full(wpq_hi), full(wpq_lo),
                  full(k1pad), full(k2pad)],
        out_specs=[pl.BlockSpec((tm, d), lambda i: (i, 0)), tab(packed_rows), tab(packed_rows), tab(PEER_KEYS),
                   tab(PEER_KEYS)],
        out_shape=[jax.ShapeDtypeStruct((n, d), MXU_DTYPE), tab_shape(packed_rows, _tab_container()),
                   tab_shape(packed_rows, _tab_container()), tab_shape(PEER_KEYS, F32), tab_shape(PEER_KEYS, F32)],
        scratch_shapes=[pltpu.VMEM((2 * PEER_HEADS, nh, PEER_KEYS, LANES), F32),
                        pltpu.VMEM((2 * PEER_HEADS, nh, PEER_KEYS, LANES), jnp.int32),
                        pltpu.VMEM((2 * PEER_HEADS, nh, PEER_TOPK, LANES), F32)],
        compiler_params=_params(("parallel",)),
        name="peer_select",
    )(x1, mod, mod, g2, wpq_hi, wpq_lo, k1pad, k2pad)


def _gelu_tanh(x):
    return x * (0.5 * (1.0 + jnp.tanh(math.sqrt(2.0 / math.pi) * (x + 0.044715 * (x * x * x)))))


def _peer_dense_kernel(h2_ref, rk2_ref, p2_ref, n1_ref, e1_ref, u0_ref, un_ref, vt_ref, x1_ref, ga_ref, o_ref,
                       acc_ref, sta_ref, stb_ref, wt_ref):
    j = pl.program_id(1)
    te, tt = sta_ref.shape

    def scores_t(u_ref, rows):
        return lax.dot_general(u_ref[rows, :], h2_ref[...], (((1,), (1,)), ((), ())), preferred_element_type=F32)

    @pl.when(j == 0)
    def _():
        acc_ref[...] = jnp.zeros_like(acc_ref)
        sta_ref[...] = scores_t(u0_ref, slice(None))

    a8 = pl.ds(pl.multiple_of(j * SUBLANES, SUBLANES), SUBLANES)
    tab = TAB_DTYPE
    kb_rows = SUBLANES * _tab_pack()
    n_kb = PEER_KEYS // kb_rows
    r_group = SUBLANES // 2
    n_tc = tt // LANES

    def sweep(st_ref, nxt_ref):
        nxt_ref[...] = scores_t(un_ref, slice(None))
        for r0 in range(0, SUBLANES, r_group):
            for tc in range(n_tc):
                lanes = slice(tc * LANES, (tc + 1) * LANES)
                coef = [[None] * r_group for _ in range(n_kb)]
                for h in range(PEER_HEADS):
                    n8, e8 = n1_ref[h, a8, lanes], e1_ref[h, a8, lanes]
                    bcast = lambda t, r: jnp.broadcast_to(t[r:r + 1], (kb_rows, LANES)).astype(tab)
                    nb = [bcast(n8, r0 + ri) for ri in range(r_group)]
                    eb = [bcast(e8, r0 + ri) for ri in range(r_group)]
                    for kb in range(n_kb):
                        rk, p = _load_packed(rk2_ref, h, kb, lanes), _load_packed(p2_ref, h, kb, lanes)
                        for ri in range(r_group):
                            term = jnp.where(rk < nb[ri], p, jnp.zeros_like(p)) * eb[ri]
                            coef[kb][ri] = term if h == 0 else coef[kb][ri] + term
                for kb in range(n_kb):
                    for ri in range(r_group):
                        first = (r0 + ri) * PEER_KEYS + kb * kb_rows
                        rows = slice(first, first + kb_rows)
                        act = _gelu_tanh(st_ref[rows, lanes]).astype(tab)
                        wt_ref[rows, lanes] = (coef[kb][ri] * act).astype(wt_ref.dtype)
            done =slice(r0 * PEER_KEYS, (r0 + r_group) * PEER_KEYS)
            acc_ref[...] += jnp.dot(vt_ref[:, done], wt_ref[done, :], preferred_element_type=F32)

    @pl.when((j & 1) == 0)
    def _():
        sweep(sta_ref, stb_ref)

    @pl.when((j & 1) == 1)
    def _():
        sweep(stb_ref, sta_ref)

    @pl.when(j == pl.num_programs(1) - 1)
    def _():
        o_ref[...] = x1_ref[...] + _mod(ga_ref) * acc_ref[...].T


def _peer_dense(h2, tabs, u, vt, x1, mod, mod_spec):
    n, d = x1.shape
    n_exp = u.shape[0]
    tt = min(512, n)
    te = SUBLANES * PEER_KEYS
    tab = lambda a: pl.BlockSpec((PEER_HEADS, a.shape[1], tt), lambda i, j: (0, 0, i))
    tok = lambda a: pl.BlockSpec((tt, a.shape[1]), lambda i, j: (i, 0))
    ms = mod_spec(tt, 5)
    ga_spec = pl.BlockSpec(ms.block_shape, lambda i, j, f=ms.index_map: f(i))
    ne = n_exp // te
    return pl.pallas_call(
        _peer_dense_kernel,
        grid=(n // tt, ne),
        in_specs=[tok(h2)] + [tab(t) for t in tabs]
                 + [pl.BlockSpec((te, d), lambda i, j: (0, 0)),
                    pl.BlockSpec((te, d), lambda i, j: ((j + 1) % ne, 0)),
                    pl.BlockSpec((d, te), lambda i, j: (0, j)),
                    tok(x1), ga_spec],
        out_specs=tok(x1),
        out_shape=jax.ShapeDtypeStruct((n, d), F32),
        scratch_shapes=[pltpu.VMEM((d, tt), F32), pltpu.VMEM((te, tt), F32), pltpu.VMEM((te, tt), F32),
                        pltpu.VMEM((te, tt), MXU_DTYPE)],
        compiler_params=_params(("parallel", "arbitrary")),
        name="peer_dense",
    )(h2, *tabs, u, u, vt, x1, mod)


def _final_kernel(x_ref, g_ref, o_ref):
    x = x_ref[...]
    o_ref[...] = x * lax.rsqrt(jnp.mean(x * x, axis=-1, keepdims=True) + EPS) * g_ref[...]


def _final_norm(x, g):
    n, d = x.shape
    tm = min(512, n)
    return pl.pallas_call(
        _final_kernel,
        grid=(n // tm,),
        in_specs=[pl.BlockSpec((tm, d), lambda i: (i, 0)), pl.BlockSpec((1, d), lambda i: (0, 0))],
        out_specs=pl.BlockSpec((tm, d), lambda i: (i, 0)),
        out_shape=jax.ShapeDtypeStruct((n, d), F32),
        compiler_params=_params(("parallel",)),
        name="final_norm",
    )(x, g)


def _shared_mod_spec(seq_len, d):
    def spec(tm, chunk):
        return pl.BlockSpec((1, 1, d), lambda i: ((i * tm) // seq_len, 0, chunk))
    return spec


def _per_token_mod_spec(d):
    def spec(tm, chunk):
        return pl.BlockSpec((tm, d), lambda i: (i, chunk))
    return spec


def kernel(x_prompt, x_sample, cache_k, cache_v, state_gla, page_table, c_prompt, c_sample, norm1_g, norm2_g, w_ada, b_ada, w_in, a_norm_g, w_sp, b_sp, w_oa, w_alpha, b_alpha, gla_norm_g, w_ob, lam_q1, lam_k1, lam_q2, lam_k2, diff_norm_g, w_oc, w_out, w_pq, sub_k1, sub_k2, peer_u, peer_v, final_g):
    depth = w_in.shape[0]
    bsz, seq, d = x_prompt.shape
    dbs, dseq, _ = x_sample.shape
    bf = lambda a: a.astype(MXU_DTYPE)

    wa = bf(w_in[:, :, 0:512])
    wb = bf(jnp.concatenate([w_in[:, :, 512:1280], jnp.pad(w_in[:, :, 1280:1296], ((0, 0), (0, 0), (0, 112)))], axis=2))
    wc = bf(w_in[:, :, 1296:2832])
    wg = bf(w_in[:, :, 2832:5904])
    wal = bf(jnp.pad(w_alpha, ((0, 0), (0, 112), (0, 0))))
    woa, wob, woc, wout = bf(w_oa), bf(w_ob), bf(w_oc), bf(w_out)
    wpq_hi = bf(w_pq)
    wpq_lo = bf(w_pq - wpq_hi.astype(F32))
    u_b = bf(peer_u)
    vt_b = bf(jnp.transpose(peer_v, (0, 2, 1)))
    half = sub_k1.shape[-1]
    hi_lo = lambda a: jnp.stack([bf(a), bf(a - bf(a).astype(F32))], axis=1)
    k1pad = hi_lo(jnp.pad(sub_k1, ((0, 0), (0, 0), (0, 0), (0, half))))
    k2pad = hi_lo(jnp.pad(sub_k2, ((0, 0), (0, 0), (0, 0), (half, 0))))
    lamp = jnp.pad(jnp.stack([lam_q1, lam_k1, lam_q2, lam_k2], axis=1), ((0, 0), (0, 0), (0, LANES - lam_q1.shape[-1])))
    gla_g = jnp.tile(gla_norm_g, (1, GLA_HEADS)).reshape(depth, 1, GLA_HEADS * GLA_DV)

    mod_all = _ada(jnp.concatenate([c_prompt, c_sample], axis=0), w_ada, b_ada)

    groups = []
    for name, x, seq_len in (("prompt", x_prompt, seq), ("sample", x_sample, dseq)):
        c = CHUNK if seq_len >= CHUNK else seq_len
        groups.append(dict(name=name, x=x.reshape(-1, d), seq_len=seq_len, c=c))

    outs = {g["name"]: dict(k=[], v=[], s=[], a=[]) for g in groups}
    for l in range(depth):
        lam_init = 0.8 - 0.6 * math.exp(-0.3 * l)
        row = lambda a: a[l].reshape(1, -1)
        for g in groups:
            n, seq_len, c = g["x"].shape[0], g["seq_len"], g["c"]
            if g["name"] == "prompt":
                mod = mod_all[l, :bsz].reshape(bsz, 1, 6 * d)
                mod_spec = _shared_mod_spec(seq_len, d)
                s0 = jnp.zeros((bsz, GLA_HEADS * GLA_DV, GLA_HEADS * GLA_DK), F32)
            else:
                mod = jnp.repeat(mod_all[l, bsz:], seq_len, axis=0)
                mod_spec = _per_token_mod_spec(d)
                s0 = _state_to_bd(state_gla[l])
            wsp_tile = jnp.tile(w_sp[l][:, :c, :c], (1, CHUNK // c, CHUNK // c))
            bsp_full = jnp.repeat(jnp.tile(b_sp[l][:, :c], (1, CHUNK // c)).T, A_WIDTH // A_GROUPS, axis=1)

            amix, avn, bq, bk, bv, br, la, cq, ck, cv = _in_proj(
                g["x"], mod, mod_spec, c, row(norm1_g), wa[l], wb[l], wc[l], wal[l], row(b_alpha), row(a_norm_g),
                wsp_tile, bsp_full)
            ob, s_new = _gla(bq, bk, bv, br, la, s0, gla_g[l], seq_len)
            if g["name"] == "prompt":
                oc = _attn_prompt(cq, ck, cv, lamp[l], row(diff_norm_g), seq_len, lam_init)
            else:
                oc = _attn_sample(cq, ck, cv, cache_k, cache_v, page_table, l, lamp[l], row(diff_norm_g), seq_len,
                                  lam_init)
            x1 = _merge(g["x"], amix, ob, oc, mod, mod_spec, row(norm1_g), woa[l], wob[l], woc[l], wg[l], wout[l])
            h2, *tabs = _peer_select(x1, mod, mod_spec, row(norm2_g), wpq_hi[l], wpq_lo[l], k1pad[l], k2pad[l])
            g["x"] = _peer_dense(h2, tabs, u_b[l], vt_b[l], x1, mod, mod_spec)

            o = outs[g["name"]]
            o["k"].append(ck)
            o["v"].append(cv)
            o["s"].append(_state_from_bd(s_new))
            o["a"].append(avn)

    fg = final_g.reshape(1, d)
    y_prompt = _final_norm(groups[0]["x"], fg).reshape(bsz, seq, d)
    y_sample = _final_norm(groups[1]["x"], fg).reshape(dbs, dseq, d)
    kv_shape = lambda b, t: (depth, b, t, DIFF_HEADS, 2 * DIFF_D)
    op, os_ = outs["prompt"], outs["sample"]
    return (y_prompt, y_sample,
            jnp.stack(op["k"]).reshape(kv_shape(bsz, seq)), jnp.stack(op["v"]).reshape(kv_shape(bsz, seq)),
            jnp.stack(op["s"]),
            jnp.stack(os_["k"]).reshape(kv_shape(dbs, dseq)), jnp.stack(os_["v"]).reshape(kv_shape(dbs, dseq)),
            jnp.stack(os_["s"]),
            jnp.stack(os_["a"]).reshape(depth, dbs, dseq, A_WIDTH))
```

```python
import functools
import math

import jax
import jax.numpy as jnp
from jax import lax
from jax.experimental import pallas as pl
from jax.experimental.pallas import tpu as pltpu

F32 = jnp.float32
MXU_DTYPE = jnp.bfloat16
TAB_DTYPE = jnp.bfloat16
EPS = 1e-6

LANES, SUBLANES = 128, 8
CHUNK = 128
A_GROUPS, A_WIDTH = 4, 256
GLA_HEADS, GLA_DK, GLA_DV = 4, 32, 64
GLA_TAU, GLA_BLOCK = 16.0, 16
DIFF_HEADS, DIFF_D, DIFF_DV = 4, 64, 128
PEER_HEADS, PEER_KEYS, PEER_TOPK = 8, 128, 16
VMEM_LIMIT = 56 * 1024 * 1024


def _mm(a, b):
    return jnp.dot(a.astype(MXU_DTYPE), b.astype(MXU_DTYPE), preferred_element_type=F32)


def _mm_nt(a, b):
    return lax.dot_general(a.astype(MXU_DTYPE), b.astype(MXU_DTYPE), (((1,), (1,)), ((), ())),
                           preferred_element_type=F32)


def _mm_f32(a, b):
    return jnp.dot(a, b, precision=lax.Precision.HIGHEST, preferred_element_type=F32)


def _iota(shape, dim):
    return lax.broadcasted_iota(jnp.int32, shape, dim)


def _mod(ref):
    return ref[0] if len(ref.shape) == 3 else ref[...]


def _rms_mod(x, g, sc, sh):
    y = x * lax.rsqrt(jnp.mean(x * x, axis=-1, keepdims=True) + EPS)
    return (y * g) * (1.0 + sc) + sh


def _params(sem, vmem=VMEM_LIMIT, flags=None):
    return pltpu.CompilerParams(dimension_semantics=sem, vmem_limit_bytes=vmem, flags=flags)


def _ada_kernel(c_ref, w_ref, b_ref, o_ref):
    c = c_ref[...]
    o_ref[0] = _mm(c * jax.nn.sigmoid(c), w_ref[0]) + b_ref[0]


def _ada(c_all, w_ada, b_ada):
    depth, d, d6 = w_ada.shape
    n = c_all.shape[0]
    tn = 1024
    return pl.pallas_call(
        _ada_kernel,
        grid=(depth, d6 // tn),
        in_specs=[pl.BlockSpec((n, d), lambda l, j: (0, 0)),
                  pl.BlockSpec((1, d, tn), lambda l, j: (l, 0, j)),
                  pl.BlockSpec((1, 1, tn), lambda l, j: (l, 0, j))],
        out_specs=pl.BlockSpec((1, n, tn), lambda l, j: (l, 0, j)),
        out_shape=jax.ShapeDtypeStruct((depth, n, d6), F32),
        compiler_params=_params(("parallel", "parallel")),
        name="ada_modulation",
    )(c_all, w_ada, b_ada.reshape(depth, 1, d6))


def _in_kernel(c, x_ref, sh_ref, sc_ref, g1_ref, wa_ref, wb_ref, wc_ref, wal_ref, bal_ref, ang_ref, wsp_ref,
               bsp_ref, amix_ref, avn_ref, bq_ref, bk_ref, bv_ref, br_ref, la_ref, cq_ref, ck_ref, cv_ref):
    tm = x_ref.shape[0]
    h = _rms_mod(x_ref[...], g1_ref[...], _mod(sc_ref), _mod(sh_ref)).astype(MXU_DTYPE)

    pc = jnp.dot(h, wc_ref[...], preferred_element_type=F32)
    cq_ref[...] = pc[:, 0:512]
    ck_ref[...] = pc[:, 512:1024]
    cv_ref[...] = pc[:, 1024:1536]

    pb = jnp.dot(h, wb_ref[...], preferred_element_type=F32)
    bq_ref[...] = pb[:, 0:128]
    bk_ref[...] = pb[:, 128:256]
    bv_ref[...] = pb[:, 256:512]
    br_ref[...] = pb[:, 512:768]
    z = _mm(pb[:, 768:896], wal_ref[...]) + bal_ref[...]
    la_ref[...] = -(jnp.maximum(-z, 0.0) + jnp.log1p(jnp.exp(-jnp.abs(z)))) * (1.0 / GLA_TAU)

    pa = jnp.dot(h, wa_ref[...], preferred_element_type=F32)
    a_u, a_v = pa[:, 0:A_WIDTH], pa[:, A_WIDTH:2 * A_WIDTH]
    vc = a_v - jnp.mean(a_v, axis=-1, keepdims=True)
    vn = vc * lax.rsqrt(jnp.mean(vc * vc, axis=-1, keepdims=True) + EPS) * ang_ref[...]
    avn_ref[...] = vn

    lc = int(math.log2(c))
    row, col = _iota((CHUNK, CHUNK), 0), _iota((CHUNK, CHUNK), 1)
    keep = ((row >> lc) == (col >> lc)) & ((col & (c - 1)) <= (row & (c - 1)))
    mst = jnp.concatenate([jnp.where(keep, wsp_ref[g], 0.0) for g in range(A_GROUPS)], axis=0).astype(MXU_DTYPE)
    lane_group = _iota((CHUNK, A_WIDTH), 1) >> 6
    bias = bsp_ref[...]
    for r in range(tm // CHUNK):
        rows = slice(r * CHUNK, (r + 1) * CHUNK)
        res = jnp.dot(mst, vn[rows].astype(MXU_DTYPE), preferred_element_type=F32)
        mixed = bias
        for g in range(A_GROUPS):
            mixed = mixed + jnp.where(lane_group == g, res[g * CHUNK:(g + 1) * CHUNK], 0.0)
        amix_ref[rows, :] = a_u[rows] * mixed


def _in_proj(x, mod, mod_spec, c, g1, wa, wb, wc, wal, bal, ang, wsp_tile, bsp_full):
    n, d = x.shape
    tm = min(512, n)
    full = lambda a: pl.BlockSpec(a.shape, lambda i: (0,) * a.ndim)
    widths = (256, 256, 128, 128, 256, 256, 128, 512, 512, 512)
    return pl.pallas_call(
        functools.partial(_in_kernel, c),
        grid=(n // tm,),
        in_specs=[pl.BlockSpec((tm, d), lambda i: (i, 0)), mod_spec(tm, 0), mod_spec(tm, 1), full(g1), full(wa),
                  full(wb), full(wc), full(wal), full(bal), full(ang), full(wsp_tile), full(bsp_full)],
        out_specs=[pl.BlockSpec((tm, w), lambda i: (i, 0)) for w in widths],
        out_shape=[jax.ShapeDtypeStruct((n, w), F32) for w in widths],
        compiler_params=_params(("parallel",)),
        name="in_proj_mixer_a",
    )(x, mod, mod, g1, wa, wb, wc, wal, bal, ang, wsp_tile, bsp_full)


def _gla_kernel(c, carry, q_ref, k_ref, v_ref, r_ref, la_ref, s0_ref, g_ref, o_ref, sout_ref, stk_ref, *st_ref):
    nb = CHUNK // c
    lc = int(math.log2(c))
    dv_all, dk_all = GLA_HEADS * GLA_DV, GLA_HEADS * GLA_DK
    la = la_ref[...]
    row, col = _iota((CHUNK, CHUNK), 0), _iota((CHUNK, CHUNK), 1)
    same = (row >> lc) == (col >> lc)
    causal = same & (col <= row)
    cum = _mm_f32(jnp.where(causal, 1.0, 0.0), la)
    tot = _mm_f32(jnp.where(same, 1.0, 0.0), la)
    k, v = k_ref[...], v_ref[...]
    qd = q_ref[...] * (GLA_DK ** -0.5) * jnp.exp(cum)
    kd = k * jnp.exp(-cum)
    kend = k * jnp.exp(tot - cum)
    dec = jnp.exp(tot)

    lane_head = _iota((CHUNK, dk_all), 1) >> 5
    qh = jnp.concatenate([jnp.where(lane_head == h, qd, 0.0) for h in range(GLA_HEADS)], axis=0)
    att = _mm_nt(qh, kd)
    att = jnp.where(jnp.concatenate([causal] * GLA_HEADS, axis=0), att, 0.0)
    res = _mm(att, v)
    v_head = _iota((CHUNK, dv_all), 1) >> 6
    o = jnp.zeros((CHUNK, dv_all), F32)
    for h in range(GLA_HEADS):
        o = o + jnp.where(v_head == h, res[h * CHUNK:(h + 1) * CHUNK], 0.0)

    vt = v.T
    t_blk = _iota((dv_all, CHUNK), 1) >> lc
    vte = jnp.concatenate([jnp.where(t_blk == n, vt, 0.0) for n in range(nb)], axis=0)
    kvt = _mm(vte, kend)
    bd = (_iota((dv_all, dk_all), 0) >> 6) == (_iota((dv_all, dk_all), 1) >> 5)

    if carry:
        st = st_ref[0]

        @pl.when(pl.program_id(1) == 0)
        def _():
            st[...] = s0_ref[0]

        s = st[...]
        for n in range(nb):
            stk_ref[n * dv_all:(n + 1) * dv_all, :] = s
            s = s * dec[n * c:n * c + 1, :] + jnp.where(bd, kvt[n * dv_all:(n + 1) * dv_all], 0.0)
        st[...] = s

        @pl.when(pl.program_id(1) == pl.num_programs(1) - 1)
        def _():
            sout_ref[0] = s
    else:
        for n in range(nb):
            s = s0_ref[n]
            stk_ref[n * dv_all:(n + 1) * dv_all, :] = s
            sout_ref[n] = s * dec[n * c:n * c + 1, :] + jnp.where(bd, kvt[n * dv_all:(n + 1) * dv_all], 0.0)

    inter = _mm_nt(qd, stk_ref[...])
    r_blk = _iota((CHUNK, dv_all), 0) >> lc
    for n in range(nb):
        o = o + jnp.where(r_blk == n, inter[:, n * dv_all:(n + 1) * dv_all], 0.0)

    gi, gj = _iota((dv_all, dv_all), 0) >> 6, _iota((dv_all, dv_all), 1) >> 6
    ms = _mm_f32(o * o, jnp.where(gi == gj, 1.0 / GLA_DV, 0.0))
    r = r_ref[...]
    o_ref[...] = o * lax.rsqrt(ms + EPS) * g_ref[...] * (r * jax.nn.sigmoid(r))


def _gla(bq, bk, bv, br, la, s0, g_tiled, seq_len):
    n = bq.shape[0]
    dv_all, dk_all = GLA_HEADS * GLA_DV, GLA_HEADS * GLA_DK
    carry = seq_len >= CHUNK
    c = GLA_BLOCK if carry else math.gcd(seq_len, GLA_BLOCK)
    nb = CHUNK // c
    if carry:
        cps = seq_len // CHUNK
        grid = (n // seq_len, cps)
        tok = lambda w: pl.BlockSpec((CHUNK, w), lambda b, ci: (b * cps + ci, 0))
        st_spec = pl.BlockSpec((1, dv_all, dk_all), lambda b, ci: (b, 0, 0))
        g_spec = pl.BlockSpec((1, dv_all), lambda b, ci: (0, 0))
        scratch = [pltpu.VMEM((nb * dv_all, dk_all), F32), pltpu.VMEM((dv_all, dk_all), F32)]
        sem = ("parallel", "arbitrary")
    else:
        assert seq_len == c and n % CHUNK == 0
        grid = (n // CHUNK,)
        tok = lambda w: pl.BlockSpec((CHUNK, w), lambda i: (i, 0))
        st_spec = pl.BlockSpec((nb, dv_all, dk_all), lambda i: (i, 0, 0))
        g_spec = pl.BlockSpec((1, dv_all), lambda i: (0, 0))
        scratch = [pltpu.VMEM((nb * dv_all, dk_all), F32)]
        sem = ("parallel",)
    return pl.pallas_call(
        functools.partial(_gla_kernel, c, carry),
        grid=grid,
        in_specs=[tok(dk_all), tok(dk_all), tok(dv_all), tok(dv_all), tok(dk_all), st_spec, g_spec],
        out_specs=[tok(dv_all), st_spec],
        out_shape=[jax.ShapeDtypeStruct((n, dv_all), F32), jax.ShapeDtypeStruct(s0.shape, F32)],
        scratch_shapes=scratch,
        compiler_params=_params(sem),
        name="gla_mixer",
    )(bq, bk, bv, br, la, s0, g_tiled)


def _state_to_bd(s):
    eye = jnp.eye(GLA_HEADS, dtype=s.dtype)
    st = jnp.transpose(s, (0, 1, 3, 2))
    return jnp.einsum('bhvk,hg->bhvgk', st, eye).reshape(s.shape[0], GLA_HEADS * GLA_DV, GLA_HEADS * GLA_DK)


def _state_from_bd(sb):
    b = sb.shape[0]
    s5 = sb.reshape(b, GLA_HEADS, GLA_DV, GLA_HEADS, GLA_DK)
    idx = jnp.arange(GLA_HEADS)
    diag = s5[:, idx, :, idx, :]
    return jnp.transpose(diag, (1, 0, 3, 2))


def _lam(lamp_ref, lam_init):
    p = lamp_ref[...]
    a = jnp.sum(p[0:1] * p[1:2], axis=-1, keepdims=True)
    b = jnp.sum(p[2:3] * p[3:4], axis=-1, keepdims=True)
    return jnp.exp(a) - jnp.exp(b) + lam_init


def _attn_prompt_kernel(tq, lam_init, q_ref, k_ref, v_ref, lamp_ref, g_ref, o_ref):
    qi = pl.program_id(2)
    lane = _iota((tq, 2 * DIFF_D), 1)
    q = q_ref[...] * (DIFF_D ** -0.5)
    q1 = jnp.where(lane < DIFF_D, q, 0.0).astype(MXU_DTYPE)
    q2 = jnp.where(lane >= DIFF_D, q, 0.0).astype(MXU_DTYPE)

    tk = 2 * tq

    def step(j, carry, masked):
        start = pl.multiple_of(j * tk, tk)
        kj = k_ref[pl.ds(start, tk), :].astype(MXU_DTYPE)
        vj = v_ref[pl.ds(start, tk), :].astype(MXU_DTYPE)
        out = []
        for qm, (m, l, acc) in zip((q1, q2), carry):
            s = lax.dot_general(qm, kj, (((1,), (1,)), ((), ())), preferred_element_type=F32)
            if masked:
                s = jnp.where(start + _iota((tq, tk), 1) <= qi * tq + _iota((tq, tk), 0), s, -jnp.inf)
            m_new = jnp.maximum(m, jnp.max(s, axis=-1, keepdims=True))
            alpha = jnp.exp(m - m_new)
            p = jnp.exp(s - m_new)
            l = alpha * l + jnp.sum(p, axis=-1, keepdims=True)
            acc = alpha * acc + jnp.dot(p.astype(MXU_DTYPE), vj, preferred_element_type=F32)
            out.append((m_new, l, acc))
        return tuple(out)

    init = tuple((jnp.full((tq, 1), -jnp.inf, F32), jnp.zeros((tq, 1), F32), jnp.zeros((tq, DIFF_DV), F32))
                 for _ in range(2))
    carry = lax.fori_loop(0, qi >> 1, lambda j, cr: step(j, cr, False), init)
    (_, l1, acc1), (_, l2, acc2) = step(qi >> 1, carry, True)
    o = acc1 / l1 - _lam(lamp_ref, lam_init) * (acc2 / l2)
    o = o * lax.rsqrt(jnp.mean(o * o, axis=-1, keepdims=True) + EPS) * g_ref[...]
    o_ref[...] = o * (1.0 - lam_init)


def _attn_prompt(cq, ck, cv, lamp, g, seq_len, lam_init):
    n = cq.shape[0]
    tq = 256
    nq = seq_len // tq
    hd = 2 * DIFF_D
    return pl.pallas_call(
        functools.partial(_attn_prompt_kernel, tq, lam_init),
        grid=(n // seq_len, DIFF_HEADS, nq),
        in_specs=[pl.BlockSpec((tq, hd), lambda b, h, qi: (b * nq + qi, h)),
                  pl.BlockSpec((seq_len, hd), lambda b, h, qi: (b, h)),
                  pl.BlockSpec((seq_len, DIFF_DV), lambda b, h, qi: (b, h)),
                  pl.BlockSpec(lamp.shape, lambda b, h, qi: (0, 0)),
                  pl.BlockSpec(g.shape, lambda b, h, qi: (0, 0))],
        out_specs=pl.BlockSpec((tq, DIFF_DV), lambda b, h, qi: (b * nq + qi, h)),
        out_shape=jax.ShapeDtypeStruct((n, DIFF_HEADS * DIFF_DV), F32),
        compiler_params=_params(("parallel", "parallel", "parallel")),
        name="diff_attn_prompt",
    )(cq, ck, cv, lamp, g)


def _attn_sample_kernel(n_pages, lam_init, pt_ref, q_ref, kn_ref, vn_ref, lamp_ref, g_ref, *refs):
    k_pages, v_pages, o_ref = refs[:n_pages], refs[n_pages:2 * n_pages], refs[2 * n_pages]
    t_new = q_ref.shape[0]
    hd_all = DIFF_HEADS * 2 * DIFF_D
    page = k_pages[0].shape[1] // DIFF_HEADS
    q = q_ref[...] * (DIFF_D ** -0.5)
    lane = _iota((t_new, hd_all), 1)
    qbd = jnp.concatenate(
        [jnp.where((lane >> 6) == 2 * h + m, q, 0.0) for m in range(2) for h in range(DIFF_HEADS)],
        axis=0).astype(MXU_DTYPE)
    nrow = qbd.shape[0]
    pad = jnp.zeros((page - t_new, hd_all), F32)
    k_new = jnp.concatenate([kn_ref[...], pad], axis=0)
    v_new = jnp.concatenate([vn_ref[...], pad], axis=0)
    rt = _iota((nrow, page), 0) & (t_new - 1)
    cj = _iota((nrow, page), 1)
    s_new = jnp.where((cj < t_new) & (cj <= rt), _mm_nt(qbd, k_new), -jnp.inf)
    heads_on_lanes = lambda ref: jnp.concatenate(
        [ref[0, pl.ds(h, page, stride=DIFF_HEADS), :] for h in range(DIFF_HEADS)], axis=-1)
    scores = [_mm_nt(qbd, heads_on_lanes(kp)) for kp in k_pages] + [s_new]
    m = functools.reduce(jnp.maximum, [jnp.max(s, axis=-1, keepdims=True) for s in scores])
    l = jnp.zeros((nrow, 1), F32)
    acc = jnp.zeros((nrow, hd_all), F32)
    for s, vr in zip(scores, list(v_pages) + [None]):
        p = jnp.exp(s - m)
        l = l + jnp.sum(p, axis=-1, keepdims=True)
        acc = acc + _mm(p, v_new if vr is None else heads_on_lanes(vr))
    acc = acc / l
    half = nrow // 2
    od = acc[0:half] - _lam(lamp_ref, lam_init) * acc[half:nrow]
    lane_head = _iota((t_new, hd_all), 1) >> 7
    o = jnp.zeros((t_new, hd_all), F32)
    for h in range(DIFF_HEADS):
        o = o + jnp.where(lane_head == h, od[h * t_new:(h + 1) * t_new], 0.0)
    g = g_ref[...]
    for h in range(DIFF_HEADS):
        oh = o[:, h * DIFF_DV:(h + 1) * DIFF_DV]
        oh = oh * lax.rsqrt(jnp.mean(oh * oh, axis=-1, keepdims=True) + EPS) * g
        o_ref[:, h * DIFF_DV:(h + 1) * DIFF_DV] = oh * (1.0 - lam_init)


def _attn_sample(cq, ck, cv, cache_k, cache_v, page_table, layer, lamp, g, t_new, lam_init):
    n = cq.shape[0]
    nbatch, n_pages = page_table.shape
    depth, n_pool, page = cache_k.shape[:3]
    hd_all = DIFF_HEADS * 2 * DIFF_D
    kc = cache_k.reshape(depth * n_pool, page * DIFF_HEADS, cache_k.shape[-1])
    vc = cache_v.reshape(depth * n_pool, page * DIFF_HEADS, cache_v.shape[-1])
    base = layer * n_pool
    tok = pl.BlockSpec((t_new, hd_all), lambda b, pt: (b, 0))
    page_spec = lambda p: pl.BlockSpec((1,) + kc.shape[1:], lambda b, pt: (base + pt[b, p], 0, 0))
    grid_spec = pltpu.PrefetchScalarGridSpec(
        num_scalar_prefetch=1,
        grid=(nbatch,),
        in_specs=[tok, tok, tok, pl.BlockSpec(lamp.shape, lambda b, pt: (0, 0)),
                  pl.BlockSpec(g.shape, lambda b, pt: (0, 0))]
                 + [page_spec(p) for p in range(n_pages)] * 2,
        out_specs=tok)
    return pl.pallas_call(
        functools.partial(_attn_sample_kernel, n_pages, lam_init),
        grid_spec=grid_spec,
        out_shape=jax.ShapeDtypeStruct((n, hd_all), F32),
        compiler_params=_params(("parallel",)),
        name="diff_attn_sample",
    )(page_table, cq, ck, cv, lamp, g, *([kc] * n_pages), *([vc] * n_pages))


def _merge_kernel(x_ref, amix_ref, ob_ref, oc_ref, sh_ref, sc_ref, ga_ref, g1_ref, woa_ref, wob_ref, woc_ref,
                  wg_ref, wout_ref, x1_ref):
    d = x_ref.shape[1]
    x = x_ref[...]
    h = _rms_mod(x, g1_ref[...], _mod(sc_ref), _mod(sh_ref)).astype(MXU_DTYPE)
    gates = jnp.dot(h, wg_ref[...], preferred_element_type=F32)
    merged = jax.nn.sigmoid(gates[:, 0:d]) * _mm(amix_ref[...], woa_ref[...])
    merged = merged + jax.nn.sigmoid(gates[:, d:2 * d]) * _mm(ob_ref[...], wob_ref[...])
    merged = merged + jax.nn.sigmoid(gates[:, 2 * d:3 * d]) * _mm(oc_ref[...], woc_ref[...])
    x1_ref[...] = x + _mod(ga_ref) * _mm(merged, wout_ref[...])


def _merge(x, amix, ob, oc, mod, mod_spec, g1, woa, wob, woc, wg, wout):
    n, d = x.shape
    tm = min(256, n)
    full = lambda a: pl.BlockSpec(a.shape, lambda i: (0,) * a.ndim)
    tok = lambda a: pl.BlockSpec((tm, a.shape[1]), lambda i: (i, 0))
    return pl.pallas_call(
        _merge_kernel,
        grid=(n // tm,),
        in_specs=[tok(x), tok(amix), tok(ob), tok(oc), mod_spec(tm, 0), mod_spec(tm, 1), mod_spec(tm, 2),
                  full(g1), full(woa), full(wob), full(woc), full(wg), full(wout)],
        out_specs=tok(x),
        out_shape=jax.ShapeDtypeStruct((n, d), F32),
        compiler_params=_params(("parallel",)),
        name="merge_out_proj",
    )(x, amix, ob, oc, mod, mod, mod, g1, woa, wob, woc, wg, wout)


_CAND_ROWS = PEER_TOPK + 7 * 8 + 8


def _tab_pack():
    return 4 // jnp.dtype(TAB_DTYPE).itemsize


def _tab_container():
    return F32 if _tab_pack() == 1 else jnp.uint32


def _store_packed(ref, h, lanes, x):
    pack = _tab_pack()
    rows = SUBLANES * pack
    for kb in range(x.shape[0] // rows):
        tile = x[kb * rows:(kb + 1) * rows].astype(TAB_DTYPE)
        ref[h, kb * SUBLANES:(kb + 1) * SUBLANES, lanes] = tile if pack == 1 else pltpu.bitcast(tile, ref.dtype)


def _load_packed(ref, h, kb, lanes):
    tile = ref[h, kb * SUBLANES:(kb + 1) * SUBLANES, lanes]
    return tile if _tab_pack() == 1 else pltpu.bitcast(tile, TAB_DTYPE)


def _peer_select_kernel(x_ref, sh_ref, sc_ref, g2_ref, wpq_hi_ref, wpq_lo_ref, k1_ref, k2_ref,
                        h2_ref, rk2_ref, p2_ref, n1_ref, e1_ref, sc_scr, rk_scr, vl_scr):
    tm = x_ref.shape[0]
    nh = tm // LANES
    lnh = int(math.log2(nh))
    h2 = _rms_mod(x_ref[...], g2_ref[...], _mod(sc_ref), _mod(sh_ref))
    h2_hi = h2.astype(MXU_DTYPE)
    h2_ref[...] = h2_hi
    h2_lo = (h2 - h2_hi.astype(F32)).astype(MXU_DTYPE)
    w_hi = wpq_hi_ref[...]
    q = (jnp.dot(h2_hi, w_hi, preferred_element_type=F32) + jnp.dot(h2_lo, w_hi, preferred_element_type=F32)
         + jnp.dot(h2_hi, wpq_lo_ref[...], preferred_element_type=F32))
    q_hi = q.astype(MXU_DTYPE)
    q_lo = (q - q_hi.astype(F32)).astype(MXU_DTYPE)
    nt = lambda a, b: lax.dot_general(a, b, (((1,), (1,)), ((), ())), preferred_element_type=F32)
    for h in range(PEER_HEADS):
        cols = slice(h * LANES, (h + 1) * LANES)
        for side, kref in enumerate((k1_ref, k2_ref)):
            k_hi, k_lo = kref[0, h], kref[1, h]
            st = nt(k_hi, q_hi[:, cols]) + nt(k_lo, q_hi[:, cols]) + nt(k_hi, q_lo[:, cols])
            for hf in range(nh):
                sc_scr[2 * h + side, hf] = st[:, hf * LANES:(hf + 1) * LANES]

    key_id = _iota((PEER_KEYS, LANES), 0)
    slot_id = _iota((PEER_TOPK, LANES), 0)

    def topk_fast(idx, bad):
        h, hf = idx >> lnh, idx & (nh - 1)
        s_a, s_b = sc_scr[2 * h, hf], sc_scr[2 * h + 1, hf]

        def it(kk, carry):
            out = []
            for s, vals in carry:
                m = jnp.max(s, axis=0, keepdims=True)
                out.append((jnp.where(s == m, -jnp.inf, s), jnp.where(slot_id == kk, m, vals)))
            return tuple(out)

        zeros = jnp.zeros((PEER_TOPK, LANES), F32)
        (_, v_a), (_, v_b) = lax.fori_loop(0, PEER_TOPK, it, ((s_a, zeros), (s_b, zeros)))
        for side, (s, vals) in enumerate(((s_a, v_a), (s_b, v_b))):
            rank = jnp.zeros((PEER_KEYS, LANES), jnp.int32)
            for kk in range(PEER_TOPK):
                rank = rank + jnp.where(vals[kk:kk + 1] > s, 1, 0)
            ranked = jnp.sum(jnp.where(rank < PEER_TOPK, 1.0, 0.0), axis=0, keepdims=True)
            bad = jnp.maximum(bad, jnp.where(ranked == PEER_TOPK, 0.0, 1.0))
            rk_scr[2 * h + side, hf] = rank
            vl_scr[2 * h + side, hf] = vals
        return bad

    bad = lax.fori_loop(0, PEER_HEADS * nh, topk_fast, jnp.zeros((1, LANES), F32))

    def topk_exact(idx, _):
        i, hf = idx >> lnh, idx & (nh - 1)

        def it(kk, carry):
            s, rank, vals = carry
            m = jnp.max(s, axis=0, keepdims=True)
            first = jnp.min(jnp.where(s == m, key_id, PEER_KEYS), axis=0, keepdims=True)
            hit = key_id == first
            return (jnp.where(hit, -jnp.inf, s), jnp.where(hit, kk, rank), jnp.where(slot_id == kk, m, vals))

        _, rank, vals = lax.fori_loop(
            0, PEER_TOPK, it,
            (sc_scr[i, hf], jnp.full((PEER_KEYS, LANES), PEER_TOPK, jnp.int32), jnp.zeros((PEER_TOPK, LANES), F32)))
        rk_scr[i, hf] = rank
        vl_scr[i, hf] = vals
        return 0

    @pl.when(jnp.max(bad) > 0.0)
    def _():
        lax.fori_loop(0, 2 * PEER_HEADS * nh, topk_exact, 0)

    r8 = _iota((8, LANES), 0)
    r16 = _iota((PEER_TOPK, LANES), 0)

    def candidates(h, hf):
        v1, v2 = vl_scr[2 * h, hf], vl_scr[2 * h + 1, hf]
        e1s, e2s = jnp.exp(v1 - v1[0:1]), jnp.exp(v2 - v2[0:1])
        blocks = [(v1[0:1] + v2, r16, e1s[0:1] * e2s)]
        for i in range(1, 8):
            ok = r8 < PEER_TOPK // (i + 1)
            blocks.append((jnp.where(ok, v1[i:i + 1] + v2[0:8], -jnp.inf), i * PEER_TOPK + r8,
                           e1s[i:i + 1] * e2s[0:8]))
        blocks.append((v1[8:16] + v2[0:1], (r8 + 8) * PEER_TOPK, e1s[8:16] * e2s[0:1]))
        return tuple(jnp.concatenate([b[k] for b in blocks], axis=0) for k in range(3))

    def final_topk(cands, flat):
        def rounds(exact):
            def it(_, carry):
                out = []
                for cnd, taken in carry:
                    m = jnp.max(cnd, axis=0, keepdims=True)
                    hit = cnd == m
                    if exact:
                        first = jnp.min(jnp.where(hit, flat, PEER_TOPK * PEER_TOPK), axis=0, keepdims=True)
                        hit = flat == first
                    out.append((jnp.where(hit, -jnp.inf, cnd), jnp.where(hit, 1.0, taken)))
                return tuple(out)

            init = tuple((c, jnp.zeros((_CAND_ROWS, LANES), F32)) for c in cands)
            return tuple(taken for _, taken in lax.fori_loop(0, PEER_TOPK, it, init))

        fast = rounds(False)
        tied = jnp.zeros((1, LANES), F32)
        for taken in fast:
            tied = jnp.maximum(tied, jnp.where(jnp.sum(taken, axis=0, keepdims=True) == PEER_TOPK, 0.0, 1.0))
        return lax.cond(jnp.max(tied) > 0.0, lambda: rounds(True), lambda: fast)

    def tables(h, hf, taken, num):
        v1, v2 = vl_scr[2 * h, hf], vl_scr[2 * h + 1, hf]
        rank1, rank2 = rk_scr[2 * h, hf], rk_scr[2 * h + 1, hf]
        z = jnp.sum(taken * num, axis=0, keepdims=True)
        counts = [jnp.sum(taken[0:16], axis=0, keepdims=True)]
        for i in range(1, 8):
            counts.append(jnp.sum(taken[8 + 8 * i:16 + 8 * i], axis=0, keepdims=True))
        tail = taken[72:80]
        n_a = jnp.zeros((PEER_KEYS, LANES), F32)
        for i in range(PEER_TOPK):
            n_i = counts[i] if i < 8 else tail[i - 8:i - 7]
            n_a = jnp.where(rank1 == i, n_i, n_a)
        lanes = slice(hf * LANES, (hf + 1) * LANES)
        s1, s2 = sc_scr[2 * h, hf], sc_scr[2 * h + 1, hf]
        n1_ref[h, :, lanes] = n_a
        e1_ref[h, :, lanes] = jnp.where(rank1 < PEER_TOPK, jnp.exp(s1 - v1[0:1]), 0.0)
        _store_packed(rk2_ref, h, lanes, rank2.astype(F32))
        _store_packed(p2_ref, h, lanes, jnp.where(rank2 < PEER_TOPK, jnp.exp(s2 - v2[0:1]), 0.0) / z)

    def per_head(h, _):
        cands = [candidates(h, hf) for hf in range(nh)]
        taken = final_topk([c[0] for c in cands], cands[0][1])
        for hf in range(nh):
            tables(h, hf, taken[hf], cands[hf][2])
        return 0

    lax.fori_loop(0, PEER_HEADS, per_head, 0)


def _peer_select(x1, mod, mod_spec, g2, wpq_hi, wpq_lo, k1pad, k2pad):
    n, d = x1.shape
    tm = min(256, n)
    nh = tm // LANES
    full = lambda a: pl.BlockSpec(a.shape, lambda i: (0,) * a.ndim)
    packed_rows = PEER_KEYS // _tab_pack()
    tab = lambda rows: pl.BlockSpec((PEER_HEADS, rows, tm), lambda i: (0, 0, i))
    tab_shape = lambda rows, dt: jax.ShapeDtypeStruct((PEER_HEADS, rows, n), dt)
    return pl.pallas_call(
        _peer_select_kernel,
        grid=(n // tm,),
        in_specs=[pl.BlockSpec((tm, d), lambda i: (i, 0)), mod_spec(tm, 3), mod_spec(tm, 4), full(g2),
                  full(wpq_hi), full(wpq_lo),
                  full(k1pad), full(k2pad)],
        out_specs=[pl.BlockSpec((tm, d), lambda i: (i, 0)), tab(packed_rows), tab(packed_rows), tab(PEER_KEYS),
                   tab(PEER_KEYS)],
        out_shape=[jax.ShapeDtypeStruct((n, d), MXU_DTYPE), tab_shape(packed_rows, _tab_container()),
                   tab_shape(packed_rows, _tab_container()), tab_shape(PEER_KEYS, F32), tab_shape(PEER_KEYS, F32)],
        scratch_shapes=[pltpu.VMEM((2 * PEER_HEADS, nh, PEER_KEYS, LANES), F32),
                        pltpu.VMEM((2 * PEER_HEADS, nh, PEER_KEYS, LANES), jnp.int32),
                        pltpu.VMEM((2 * PEER_HEADS, nh, PEER_TOPK, LANES), F32)],
        compiler_params=_params(("parallel",)),
        name="peer_select",
    )(x1, mod, mod, g2, wpq_hi, wpq_lo, k1pad, k2pad)


def _gelu_tanh(x):
    return x * (0.5 * (1.0 + jnp.tanh(math.sqrt(2.0 / math.pi) * (x + 0.044715 * (x * x * x)))))


def _peer_dense_kernel(h2_ref, rk2_ref, p2_ref, n1_ref, e1_ref, u0_ref, un_ref, vt_ref, x1_ref, ga_ref, o_ref,
                       acc_ref, sta_ref, stb_ref, wt_ref):
    j = pl.program_id(1)
    te, tt = sta_ref.shape

    def scores_t(u_ref, rows):
        return lax.dot_general(u_ref[rows, :], h2_ref[...], (((1,), (1,)), ((), ())), preferred_element_type=F32)

    @pl.when(j == 0)
    def _():
        acc_ref[...] = jnp.zeros_like(acc_ref)
        sta_ref[...] = scores_t(u0_ref, slice(None))

    a8 = pl.ds(pl.multiple_of(j * SUBLANES, SUBLANES), SUBLANES)
    tab = TAB_DTYPE
    kb_rows = SUBLANES * _tab_pack()
    n_kb = PEER_KEYS // kb_rows
    r_group = SUBLANES // 2
    n_tc = tt // LANES

    def sweep(st_ref, nxt_ref):
        nxt_ref[...] = scores_t(un_ref, slice(None))
        for r0 in range(0, SUBLANES, r_group):
            for tc in range(n_tc):
                lanes = slice(tc * LANES, (tc + 1) * LANES)
                coef = [[None] * r_group for _ in range(n_kb)]
                for h in range(PEER_HEADS):
                    n8, e8 = n1_ref[h, a8, lanes], e1_ref[h, a8, lanes]
                    bcast = lambda t, r: jnp.broadcast_to(t[r:r + 1], (kb_rows, LANES)).astype(tab)
                    nb = [bcast(n8, r0 + ri) for ri in range(r_group)]
                    eb = [bcast(e8, r0 + ri) for ri in range(r_group)]
                    for kb in range(n_kb):
                        rk, p = _load_packed(rk2_ref, h, kb, lanes), _load_packed(p2_ref, h, kb, lanes)
                        for ri in range(r_group):
                            term = jnp.where(rk < nb[ri], p, jnp.zeros_like(p)) * eb[ri]
                            coef[kb][ri] = term if h == 0 else coef[kb][ri] + term
                for kb in range(n_kb):
                    for ri in range(r_group):
                        first = (r0 + ri) * PEER_KEYS + kb * kb_rows
                        rows = slice(first, first + kb_rows)
                        act = _gelu_tanh(st_ref[rows, lanes]).astype(tab)
                        wt_ref[rows, lanes] = (coef[kb][ri] * act).astype(wt_ref.dtype)
            done =slice(r0 * PEER_KEYS, (r0 + r_group) * PEER_KEYS)
            acc_ref[...] += jnp.dot(vt_ref[:, done], wt_ref[done, :], preferred_element_type=F32)

    @pl.when((j & 1) == 0)
    def _():
        sweep(sta_ref, stb_ref)

    @pl.when((j & 1) == 1)
    def _():
        sweep(stb_ref, sta_ref)

    @pl.when(j == pl.num_programs(1) - 1)
    def _():
        o_ref[...] = x1_ref[...] + _mod(ga_ref) * acc_ref[...].T


def _peer_dense(h2, tabs, u, vt, x1, mod, mod_spec):
    n, d = x1.shape
    n_exp = u.shape[0]
    tt = min(512, n)
    te = SUBLANES * PEER_KEYS
    tab = lambda a: pl.BlockSpec((PEER_HEADS, a.shape[1], tt), lambda i, j: (0, 0, i))
    tok = lambda a: pl.BlockSpec((tt, a.shape[1]), lambda i, j: (i, 0))
    ms = mod_spec(tt, 5)
    ga_spec = pl.BlockSpec(ms.block_shape, lambda i, j, f=ms.index_map: f(i))
    ne = n_exp // te
    return pl.pallas_call(
        _peer_dense_kernel,
        grid=(n // tt, ne),
        in_specs=[tok(h2)] + [tab(t) for t in tabs]
                 + [pl.BlockSpec((te, d), lambda i, j: (0, 0)),
                    pl.BlockSpec((te, d), lambda i, j: ((j + 1) % ne, 0)),
                    pl.BlockSpec((d, te), lambda i, j: (0, j)),
                    tok(x1), ga_spec],
        out_specs=tok(x1),
        out_shape=jax.ShapeDtypeStruct((n, d), F32),
        scratch_shapes=[pltpu.VMEM((d, tt), F32), pltpu.VMEM((te, tt), F32), pltpu.VMEM((te, tt), F32),
                        pltpu.VMEM((te, tt), MXU_DTYPE)],
        compiler_params=_params(("parallel", "arbitrary")),
        name="peer_dense",
    )(h2, *tabs, u, u, vt, x1, mod)


def _final_kernel(x_ref, g_ref, o_ref):
    x = x_ref[...]
    o_ref[...] = x * lax.rsqrt(jnp.mean(x * x, axis=-1, keepdims=True) + EPS) * g_ref[...]


def _final_norm(x, g):
    n, d = x.shape
    tm = min(512, n)
    return pl.pallas_call(
        _final_kernel,
        grid=(n // tm,),
        in_specs=[pl.BlockSpec((tm, d), lambda i: (i, 0)), pl.BlockSpec((1, d), lambda i: (0, 0))],
        out_specs=pl.BlockSpec((tm, d), lambda i: (i, 0)),
        out_shape=jax.ShapeDtypeStruct((n, d), F32),
        compiler_params=_params(("parallel",)),
        name="final_norm",
    )(x, g)


def _shared_mod_spec(seq_len, d):
    def spec(tm, chunk):
        return pl.BlockSpec((1, 1, d), lambda i: ((i * tm) // seq_len, 0, chunk))
    return spec


def _per_token_mod_spec(d):
    def spec(tm, chunk):
        return pl.BlockSpec((tm, d), lambda i: (i, chunk))
    return spec


def kernel(x_prompt, x_sample, cache_k, cache_v, state_gla, page_table, c_prompt, c_sample, norm1_g, norm2_g, w_ada, b_ada, w_in, a_norm_g, w_sp, b_sp, w_oa, w_alpha, b_alpha, gla_norm_g, w_ob, lam_q1, lam_k1, lam_q2, lam_k2, diff_norm_g, w_oc, w_out, w_pq, sub_k1, sub_k2, peer_u, peer_v, final_g):
    depth = w_in.shape[0]
    bsz, seq, d = x_prompt.shape
    dbs, dseq, _ = x_sample.shape
    bf = lambda a: a.astype(MXU_DTYPE)

    wa = bf(w_in[:, :, 0:512])
    wb = bf(jnp.concatenate([w_in[:, :, 512:1280], jnp.pad(w_in[:, :, 1280:1296], ((0, 0), (0, 0), (0, 112)))], axis=2))
    wc = bf(w_in[:, :, 1296:2832])
    wg = bf(w_in[:, :, 2832:5904])
    wal = bf(jnp.pad(w_alpha, ((0, 0), (0, 112), (0, 0))))
    woa, wob, woc, wout = bf(w_oa), bf(w_ob), bf(w_oc), bf(w_out)
    wpq_hi = bf(w_pq)
    wpq_lo = bf(w_pq - wpq_hi.astype(F32))
    u_b = bf(peer_u)
    vt_b = bf(jnp.transpose(peer_v, (0, 2, 1)))
    half = sub_k1.shape[-1]
    hi_lo = lambda a: jnp.stack([bf(a), bf(a - bf(a).astype(F32))], axis=1)
    k1pad = hi_lo(jnp.pad(sub_k1, ((0, 0), (0, 0), (0, 0), (0, half))))
    k2pad = hi_lo(jnp.pad(sub_k2, ((0, 0), (0, 0), (0, 0), (half, 0))))
    lamp = jnp.pad(jnp.stack([lam_q1, lam_k1, lam_q2, lam_k2], axis=1), ((0, 0), (0, 0), (0, LANES - lam_q1.shape[-1])))
    gla_g = jnp.tile(gla_norm_g, (1, GLA_HEADS)).reshape(depth, 1, GLA_HEADS * GLA_DV)

    mod_all = _ada(jnp.concatenate([c_prompt, c_sample], axis=0), w_ada, b_ada)

    groups = []
    for name, x, seq_len in (("prompt", x_prompt, seq), ("sample", x_sample, dseq)):
        c = CHUNK if seq_len >= CHUNK else seq_len
        groups.append(dict(name=name, x=x.reshape(-1, d), seq_len=seq_len, c=c))

    outs = {g["name"]: dict(k=[], v=[], s=[], a=[]) for g in groups}
    for l in range(depth):
        lam_init = 0.8 - 0.6 * math.exp(-0.3 * l)
        row = lambda a: a[l].reshape(1, -1)
        for g in groups:
            n, seq_len, c = g["x"].shape[0], g["seq_len"], g["c"]
            if g["name"] == "prompt":
                mod = mod_all[l, :bsz].reshape(bsz, 1, 6 * d)
                mod_spec = _shared_mod_spec(seq_len, d)
                s0 = jnp.zeros((bsz, GLA_HEADS * GLA_DV, GLA_HEADS * GLA_DK), F32)
            else:
                mod = jnp.repeat(mod_all[l, bsz:], seq_len, axis=0)
                mod_spec = _per_token_mod_spec(d)
                s0 = _state_to_bd(state_gla[l])
            wsp_tile = jnp.tile(w_sp[l][:, :c, :c], (1, CHUNK // c, CHUNK // c))
            bsp_full = jnp.repeat(jnp.tile(b_sp[l][:, :c], (1, CHUNK // c)).T, A_WIDTH // A_GROUPS, axis=1)

            amix, avn, bq, bk, bv, br, la, cq, ck, cv = _in_proj(
                g["x"], mod, mod_spec, c, row(norm1_g), wa[l], wb[l], wc[l], wal[l], row(b_alpha), row(a_norm_g),
                wsp_tile, bsp_full)
            ob, s_new = _gla(bq, bk, bv, br, la, s0, gla_g[l], seq_len)
            if g["name"] == "prompt":
                oc = _attn_prompt(cq, ck, cv, lamp[l], row(diff_norm_g), seq_len, lam_init)
            else:
                oc = _attn_sample(cq, ck, cv, cache_k, cache_v, page_table, l, lamp[l], row(diff_norm_g), seq_len,
                                  lam_init)
            x1 = _merge(g["x"], amix, ob, oc, mod, mod_spec, row(norm1_g), woa[l], wob[l], woc[l], wg[l], wout[l])
            h2, *tabs = _peer_select(x1, mod, mod_spec, row(norm2_g), wpq_hi[l], wpq_lo[l], k1pad[l], k2pad[l])
            g["x"] = _peer_dense(h2, tabs, u_b[l], vt_b[l], x1, mod, mod_spec)

            o = outs[g["name"]]
            o["k"].append(ck)
            o["v"].append(cv)
            o["s"].append(_state_from_bd(s_new))
            o["a"].append(avn)

    fg = final_g.reshape(1, d)
    y_prompt = _final_norm(groups[0]["x"], fg).reshape(bsz, seq, d)
    y_sample = _final_norm(groups[1]["x"], fg).reshape(dbs, dseq, d)
    kv_shape = lambda b, t: (depth, b, t, DIFF_HEADS, 2 * DIFF_D)
    op, os_ = outs["prompt"], outs["sample"]
    return (y_prompt, y_sample,
            jnp.stack(op["k"]).reshape(kv_shape(bsz, seq)), jnp.stack(op["v"]).reshape(kv_shape(bsz, seq)),
            jnp.stack(op["s"]),
            jnp.stack(os_["k"]).reshape(kv_shape(dbs, dseq)), jnp.stack(os_["v"]).reshape(kv_shape(dbs, dseq)),
            jnp.stack(os_["s"]),
            jnp.stack(os_["a"]).reshape(depth, dbs, dseq, A_WIDTH))
```

```python
import functools
import math

import jax
import jax.numpy as jnp
from jax import lax
from jax.experimental import pallas as pl
from jax.experimental.pallas import tpu as pltpu

F32 = jnp.float32
MXU_DTYPE = jnp.bfloat16
TAB_DTYPE = jnp.bfloat16
EPS = 1e-6

LANES, SUBLANES = 128, 8
CHUNK = 128
A_GROUPS, A_WIDTH = 4, 256
GLA_HEADS, GLA_DK, GLA_DV = 4, 32, 64
GLA_TAU, GLA_BLOCK = 16.0, 16
DIFF_HEADS, DIFF_D, DIFF_DV = 4, 64, 128
PEER_HEADS, PEER_KEYS, PEER_TOPK = 8, 128, 16
ATTN_Q_TILE, ATTN_KEY_TILES = 512, 1
VMEM_LIMIT = 56 * 1024 * 1024


def _mm(a, b):
    return jnp.dot(a.astype(MXU_DTYPE), b.astype(MXU_DTYPE), preferred_element_type=F32)


def _mm_nt(a, b):
    return lax.dot_general(a.astype(MXU_DTYPE), b.astype(MXU_DTYPE), (((1,), (1,)), ((), ())),
                           preferred_element_type=F32)


def _mm_f32(a, b):
    return jnp.dot(a, b, precision=lax.Precision.HIGHEST, preferred_element_type=F32)


def _iota(shape, dim):
    return lax.broadcasted_iota(jnp.int32, shape, dim)


def _mod(ref):
    return ref[0] if len(ref.shape) == 3 else ref[...]


def _rms_mod(x, g, sc, sh):
    y = x * lax.rsqrt(jnp.mean(x * x, axis=-1, keepdims=True) + EPS)
    return (y * g) * (1.0 + sc) + sh


def _params(sem, vmem=VMEM_LIMIT, flags=None):
    return pltpu.CompilerParams(dimension_semantics=sem, vmem_limit_bytes=vmem, flags=flags)


def _ada_kernel(c_ref, w_ref, b_ref, o_ref):
    c = c_ref[...]
    o_ref[0] = _mm(c * jax.nn.sigmoid(c), w_ref[0]) + b_ref[0]


def _ada(c_all, w_ada, b_ada):
    depth, d, d6 = w_ada.shape
    n = c_all.shape[0]
    tn = 1024
    return pl.pallas_call(
        _ada_kernel,
        grid=(depth, d6 // tn),
        in_specs=[pl.BlockSpec((n, d), lambda l, j: (0, 0)),
                  pl.BlockSpec((1, d, tn), lambda l, j: (l, 0, j)),
                  pl.BlockSpec((1, 1, tn), lambda l, j: (l, 0, j))],
        out_specs=pl.BlockSpec((1, n, tn), lambda l, j: (l, 0, j)),
        out_shape=jax.ShapeDtypeStruct((depth, n, d6), F32),
        compiler_params=_params(("parallel", "parallel")),
        name="ada_modulation",
    )(c_all, w_ada, b_ada.reshape(depth, 1, d6))


def _in_kernel(c, x_ref, sh_ref, sc_ref, g1_ref, wa_ref, wb_ref, wc_ref, wal_ref, bal_ref, ang_ref, wsp_ref,
               bsp_ref, amix_ref, avn_ref, bq_ref, bk_ref, bv_ref, br_ref, la_ref, cq_ref, ck_ref, cv_ref):
    tm = x_ref.shape[0]
    h = _rms_mod(x_ref[...], g1_ref[...], _mod(sc_ref), _mod(sh_ref)).astype(MXU_DTYPE)

    pc = jnp.dot(h, wc_ref[...], preferred_element_type=F32)
    cq_ref[...] = pc[:, 0:512]
    ck_ref[...] = pc[:, 512:1024]
    cv_ref[...] = pc[:, 1024:1536]

    pb = jnp.dot(h, wb_ref[...], preferred_element_type=F32)
    bq_ref[...] = pb[:, 0:128]
    bk_ref[...] = pb[:, 128:256]
    bv_ref[...] = pb[:, 256:512]
    br_ref[...] = pb[:, 512:768]
    z = _mm(pb[:, 768:896], wal_ref[...]) + bal_ref[...]
    la_ref[...] = -(jnp.maximum(-z, 0.0) + jnp.log1p(jnp.exp(-jnp.abs(z)))) * (1.0 / GLA_TAU)

    pa = jnp.dot(h, wa_ref[...], preferred_element_type=F32)
    a_u, a_v = pa[:, 0:A_WIDTH], pa[:, A_WIDTH:2 * A_WIDTH]
    vc = a_v - jnp.mean(a_v, axis=-1, keepdims=True)
    vn = vc * lax.rsqrt(jnp.mean(vc * vc, axis=-1, keepdims=True) + EPS) * ang_ref[...]
    avn_ref[...] = vn

    lc = int(math.log2(c))
    row, col = _iota((CHUNK, CHUNK), 0), _iota((CHUNK, CHUNK), 1)
    keep = ((row >> lc) == (col >> lc)) & ((col & (c - 1)) <= (row & (c - 1)))
    mst = jnp.concatenate([jnp.where(keep, wsp_ref[g], 0.0) for g in range(A_GROUPS)], axis=0).astype(MXU_DTYPE)
    lane_group = _iota((CHUNK, A_WIDTH), 1) >> 6
    bias = bsp_ref[...]
    for r in range(tm // CHUNK):
        rows = slice(r * CHUNK, (r + 1) * CHUNK)
        res = jnp.dot(mst, vn[rows].astype(MXU_DTYPE), preferred_element_type=F32)
        mixed = bias
        for g in range(A_GROUPS):
            mixed = mixed + jnp.where(lane_group == g, res[g * CHUNK:(g + 1) * CHUNK], 0.0)
        amix_ref[rows, :] = a_u[rows] * mixed


def _in_proj(x, mod, mod_spec, c, g1, wa, wb, wc, wal, bal, ang, wsp_tile, bsp_full):
    n, d = x.shape
    tm = min(512, n)
    full = lambda a: pl.BlockSpec(a.shape, lambda i: (0,) * a.ndim)
    widths = (256, 256, 128, 128, 256, 256, 128, 512, 512, 512)
    return pl.pallas_call(
        functools.partial(_in_kernel, c),
        grid=(n // tm,),
        in_specs=[pl.BlockSpec((tm, d), lambda i: (i, 0)), mod_spec(tm, 0), mod_spec(tm, 1), full(g1), full(wa),
                  full(wb), full(wc), full(wal), full(bal), full(ang), full(wsp_tile), full(bsp_full)],
        out_specs=[pl.BlockSpec((tm, w), lambda i: (i, 0)) for w in widths],
        out_shape=[jax.ShapeDtypeStruct((n, w), F32) for w in widths],
        compiler_params=_params(("parallel",)),
        name="in_proj_mixer_a",
    )(x, mod, mod, g1, wa, wb, wc, wal, bal, ang, wsp_tile, bsp_full)


def _gla_kernel(c, carry, q_ref, k_ref, v_ref, r_ref, la_ref, s0_ref, g_ref, o_ref, sout_ref, stk_ref, *st_ref):
    nb = CHUNK // c
    lc = int(math.log2(c))
    dv_all, dk_all = GLA_HEADS * GLA_DV, GLA_HEADS * GLA_DK
    la = la_ref[...]
    row, col = _iota((CHUNK, CHUNK), 0), _iota((CHUNK, CHUNK), 1)
    same = (row >> lc) == (col >> lc)
    causal = same & (col <= row)
    cum = _mm_f32(jnp.where(causal, 1.0, 0.0), la)
    tot = _mm_f32(jnp.where(same, 1.0, 0.0), la)
    k, v = k_ref[...], v_ref[...]
    qd = q_ref[...] * (GLA_DK ** -0.5) * jnp.exp(cum)
    kd = k * jnp.exp(-cum)
    kend = k * jnp.exp(tot - cum)
    dec = jnp.exp(tot)

    lane_head = _iota((CHUNK, dk_all), 1) >> 5
    qh = jnp.concatenate([jnp.where(lane_head == h, qd, 0.0) for h in range(GLA_HEADS)], axis=0)
    att = _mm_nt(qh, kd)
    att = jnp.where(jnp.concatenate([causal] * GLA_HEADS, axis=0), att, 0.0)
    res = _mm(att, v)
    v_head = _iota((CHUNK, dv_all), 1) >> 6
    o = jnp.zeros((CHUNK, dv_all), F32)
    for h in range(GLA_HEADS):
        o = o + jnp.where(v_head == h, res[h * CHUNK:(h + 1) * CHUNK], 0.0)

    vt = v.T
    t_blk = _iota((dv_all, CHUNK), 1) >> lc
    vte = jnp.concatenate([jnp.where(t_blk == n, vt, 0.0) for n in range(nb)], axis=0)
    kvt = _mm(vte, kend)
    bd = (_iota((dv_all, dk_all), 0) >> 6) == (_iota((dv_all, dk_all), 1) >> 5)

    if carry:
        st = st_ref[0]

        @pl.when(pl.program_id(1) == 0)
        def _():
            st[...] = s0_ref[0]

        s = st[...]
        for n in range(nb):
            stk_ref[n * dv_all:(n + 1) * dv_all, :] = s
            s = s * dec[n * c:n * c + 1, :] + jnp.where(bd, kvt[n * dv_all:(n + 1) * dv_all], 0.0)
        st[...] = s

        @pl.when(pl.program_id(1) == pl.num_programs(1) - 1)
        def _():
            sout_ref[0] = s
    else:
        for n in range(nb):
            s = s0_ref[n]
            stk_ref[n * dv_all:(n + 1) * dv_all, :] = s
            sout_ref[n] = s * dec[n * c:n * c + 1, :] + jnp.where(bd, kvt[n * dv_all:(n + 1) * dv_all], 0.0)

    inter = _mm_nt(qd, stk_ref[...])
    r_blk = _iota((CHUNK, dv_all), 0) >> lc
    for n in range(nb):
        o = o + jnp.where(r_blk == n, inter[:, n * dv_all:(n + 1) * dv_all], 0.0)

    gi, gj = _iota((dv_all, dv_all), 0) >> 6, _iota((dv_all, dv_all), 1) >> 6
    ms = _mm_f32(o * o, jnp.where(gi == gj, 1.0 / GLA_DV, 0.0))
    r = r_ref[...]
    o_ref[...] = o * lax.rsqrt(ms + EPS) * g_ref[...] * (r * jax.nn.sigmoid(r))


def _gla(bq, bk, bv, br, la, s0, g_tiled, seq_len):
    n = bq.shape[0]
    dv_all, dk_all = GLA_HEADS * GLA_DV, GLA_HEADS * GLA_DK
    carry = seq_len >= CHUNK
    c = GLA_BLOCK if carry else math.gcd(seq_len, GLA_BLOCK)
    nb = CHUNK // c
    if carry:
        cps = seq_len // CHUNK
        grid = (n // seq_len, cps)
        tok = lambda w: pl.BlockSpec((CHUNK, w), lambda b, ci: (b * cps + ci, 0))
        st_spec = pl.BlockSpec((1, dv_all, dk_all), lambda b, ci: (b, 0, 0))
        g_spec = pl.BlockSpec((1, dv_all), lambda b, ci: (0, 0))
        scratch = [pltpu.VMEM((nb * dv_all, dk_all), F32), pltpu.VMEM((dv_all, dk_all), F32)]
        sem = ("parallel", "arbitrary")
    else:
        assert seq_len == c and n % CHUNK == 0
        grid = (n // CHUNK,)
        tok = lambda w: pl.BlockSpec((CHUNK, w), lambda i: (i, 0))
        st_spec = pl.BlockSpec((nb, dv_all, dk_all), lambda i: (i, 0, 0))
        g_spec = pl.BlockSpec((1, dv_all), lambda i: (0, 0))
        scratch = [pltpu.VMEM((nb * dv_all, dk_all), F32)]
        sem = ("parallel",)
    return pl.pallas_call(
        functools.partial(_gla_kernel, c, carry),
        grid=grid,
        in_specs=[tok(dk_all), tok(dk_all), tok(dv_all), tok(dv_all), tok(dk_all), st_spec, g_spec],
        out_specs=[tok(dv_all), st_spec],
        out_shape=[jax.ShapeDtypeStruct((n, dv_all), F32), jax.ShapeDtypeStruct(s0.shape, F32)],
        scratch_shapes=scratch,
        compiler_params=_params(sem),
        name="gla_mixer",
    )(bq, bk, bv, br, la, s0, g_tiled)


def _state_to_bd(s):
    eye = jnp.eye(GLA_HEADS, dtype=s.dtype)
    st = jnp.transpose(s, (0, 1, 3, 2))
    return jnp.einsum('bhvk,hg->bhvgk', st, eye).reshape(s.shape[0], GLA_HEADS * GLA_DV, GLA_HEADS * GLA_DK)


def _state_from_bd(sb):
    b = sb.shape[0]
    s5 = sb.reshape(b, GLA_HEADS, GLA_DV, GLA_HEADS, GLA_DK)
    idx = jnp.arange(GLA_HEADS)
    diag = s5[:, idx, :, idx, :]
    return jnp.transpose(diag, (1, 0, 3, 2))


def _lam(lamp_ref, lam_init):
    p = lamp_ref[...]
    a = jnp.sum(p[0:1] * p[1:2], axis=-1, keepdims=True)
    b = jnp.sum(p[2:3] * p[3:4], axis=-1, keepdims=True)
    return jnp.exp(a) - jnp.exp(b) + lam_init


def _attn_prompt_kernel(tq, lam_init, q_ref, k_ref, v_ref, lamp_ref, g_ref, o_ref):
    qi = pl.program_id(2)
    lane = _iota((tq, 2 * DIFF_D), 1)
    q = q_ref[...] * (DIFF_D ** -0.5)
    q1 = jnp.where(lane < DIFF_D, q, 0.0).astype(MXU_DTYPE)
    q2 = jnp.where(lane >= DIFF_D, q, 0.0).astype(MXU_DTYPE)

    tk = ATTN_KEY_TILES * tq

    def step(j, carry, masked):
        start = pl.multiple_of(j * tk, tk)
        kj = k_ref[pl.ds(start, tk), :].astype(MXU_DTYPE)
        vj = v_ref[pl.ds(start, tk), :].astype(MXU_DTYPE)
        out = []
        for qm, (m, l, acc) in zip((q1, q2), carry):
            s = lax.dot_general(qm, kj, (((1,), (1,)), ((), ())), preferred_element_type=F32)
            if masked:
                s = jnp.where(start + _iota((tq, tk), 1) <= qi * tq + _iota((tq, tk), 0), s, -jnp.inf)
            m_new = jnp.maximum(m, jnp.max(s, axis=-1, keepdims=True))
            alpha = jnp.exp(m - m_new)
            p = jnp.exp(s - m_new)
            l = alpha * l + jnp.sum(p, axis=-1, keepdims=True)
            acc = alpha * acc + jnp.dot(p.astype(MXU_DTYPE), vj, preferred_element_type=F32)
            out.append((m_new, l, acc))
        return tuple(out)

    init = tuple((jnp.full((tq, 1), -jnp.inf, F32), jnp.zeros((tq, 1), F32), jnp.zeros((tq, DIFF_DV), F32))
                 for _ in range(2))
    n_full = qi >> int(math.log2(ATTN_KEY_TILES))
    carry = lax.fori_loop(0, n_full, lambda j, cr: step(j, cr, False), init)
    (_, l1, acc1), (_, l2, acc2) = step(n_full, carry, True)
    o = acc1 / l1 - _lam(lamp_ref, lam_init) * (acc2 / l2)
    o = o * lax.rsqrt(jnp.mean(o * o, axis=-1, keepdims=True) + EPS) * g_ref[...]
    o_ref[...] = o * (1.0 - lam_init)


def _attn_prompt(cq, ck, cv, lamp, g, seq_len, lam_init):
    n = cq.shape[0]
    tq = ATTN_Q_TILE
    nq = seq_len // tq
    hd = 2 * DIFF_D
    return pl.pallas_call(
        functools.partial(_attn_prompt_kernel, tq, lam_init),
        grid=(n // seq_len, DIFF_HEADS, nq),
        in_specs=[pl.BlockSpec((tq, hd), lambda b, h, qi: (b * nq + qi, h)),
                  pl.BlockSpec((seq_len, hd), lambda b, h, qi: (b, h)),
                  pl.BlockSpec((seq_len, DIFF_DV), lambda b, h, qi: (b, h)),
                  pl.BlockSpec(lamp.shape, lambda b, h, qi: (0, 0)),
                  pl.BlockSpec(g.shape, lambda b, h, qi: (0, 0))],
        out_specs=pl.BlockSpec((tq, DIFF_DV), lambda b, h, qi: (b * nq + qi, h)),
        out_shape=jax.ShapeDtypeStruct((n, DIFF_HEADS * DIFF_DV), F32),
        compiler_params=_params(("parallel", "parallel", "parallel")),
        name="diff_attn_prompt",
    )(cq, ck, cv, lamp, g)


def _attn_sample_kernel(n_pages, lam_init, pt_ref, q_ref, kn_ref, vn_ref, lamp_ref, g_ref, *refs):
    k_pages, v_pages, o_ref = refs[:n_pages], refs[n_pages:2 * n_pages], refs[2 * n_pages]
    t_new = q_ref.shape[0]
    hd_all = DIFF_HEADS * 2 * DIFF_D
    page = k_pages[0].shape[1] // DIFF_HEADS
    q = q_ref[...] * (DIFF_D ** -0.5)
    lane = _iota((t_new, hd_all), 1)
    qbd = jnp.concatenate(
        [jnp.where((lane >> 6) == 2 * h + m, q, 0.0) for m in range(2) for h in range(DIFF_HEADS)],
        axis=0).astype(MXU_DTYPE)
    nrow = qbd.shape[0]
    pad = jnp.zeros((page - t_new, hd_all), F32)
    k_new = jnp.concatenate([kn_ref[...], pad], axis=0)
    v_new = jnp.concatenate([vn_ref[...], pad], axis=0)
    rt = _iota((nrow, page), 0) & (t_new - 1)
    cj = _iota((nrow, page), 1)
    s_new = jnp.where((cj < t_new) & (cj <= rt), _mm_nt(qbd, k_new), -jnp.inf)
    heads_on_lanes = lambda ref: jnp.concatenate(
        [ref[0, pl.ds(h, page, stride=DIFF_HEADS), :] for h in range(DIFF_HEADS)], axis=-1)
    scores = [_mm_nt(qbd, heads_on_lanes(kp)) for kp in k_pages] + [s_new]
    m = functools.reduce(jnp.maximum, [jnp.max(s, axis=-1, keepdims=True) for s in scores])
    l = jnp.zeros((nrow, 1), F32)
    acc = jnp.zeros((nrow, hd_all), F32)
    for s, vr in zip(scores, list(v_pages) + [None]):
        p = jnp.exp(s - m)
        l = l + jnp.sum(p, axis=-1, keepdims=True)
        acc = acc + _mm(p, v_new if vr is None else heads_on_lanes(vr))
    acc = acc / l
    half = nrow // 2
    od = acc[0:half] - _lam(lamp_ref, lam_init) * acc[half:nrow]
    lane_head = _iota((t_new, hd_all), 1) >> 7
    o = jnp.zeros((t_new, hd_all), F32)
    for h in range(DIFF_HEADS):
        o = o + jnp.where(lane_head == h, od[h * t_new:(h + 1) * t_new], 0.0)
    g = g_ref[...]
    for h in range(DIFF_HEADS):
        oh = o[:, h * DIFF_DV:(h + 1) * DIFF_DV]
        oh = oh * lax.rsqrt(jnp.mean(oh * oh, axis=-1, keepdims=True) + EPS) * g
        o_ref[:, h * DIFF_DV:(h + 1) * DIFF_DV] = oh * (1.0 - lam_init)


def _attn_sample(cq, ck, cv, cache_k, cache_v, page_table, layer, lamp, g, t_new, lam_init):
    n = cq.shape[0]
    nbatch, n_pages = page_table.shape
    depth, n_pool, page = cache_k.shape[:3]
    hd_all = DIFF_HEADS * 2 * DIFF_D
    kc = cache_k.reshape(depth * n_pool, page * DIFF_HEADS, cache_k.shape[-1])
    vc = cache_v.reshape(depth * n_pool, page * DIFF_HEADS, cache_v.shape[-1])
    base = layer * n_pool
    tok = pl.BlockSpec((t_new, hd_all), lambda b, pt: (b, 0))
    page_spec = lambda p: pl.BlockSpec((1,) + kc.shape[1:], lambda b, pt: (base + pt[b, p], 0, 0))
    grid_spec = pltpu.PrefetchScalarGridSpec(
        num_scalar_prefetch=1,
        grid=(nbatch,),
        in_specs=[tok, tok, tok, pl.BlockSpec(lamp.shape, lambda b, pt: (0, 0)),
                  pl.BlockSpec(g.shape, lambda b, pt: (0, 0))]
                 + [page_spec(p) for p in range(n_pages)] * 2,
        out_specs=tok)
    return pl.pallas_call(
        functools.partial(_attn_sample_kernel, n_pages, lam_init),
        grid_spec=grid_spec,
        out_shape=jax.ShapeDtypeStruct((n, hd_all), F32),
        compiler_params=_params(("parallel",)),
        name="diff_attn_sample",
    )(page_table, cq, ck, cv, lamp, g, *([kc] * n_pages), *([vc] * n_pages))


def _merge_kernel(x_ref, amix_ref, ob_ref, oc_ref, sh_ref, sc_ref, ga_ref, g1_ref, woa_ref, wob_ref, woc_ref,
                  wg_ref, wout_ref, x1_ref):
    d = x_ref.shape[1]
    x = x_ref[...]
    h = _rms_mod(x, g1_ref[...], _mod(sc_ref), _mod(sh_ref)).astype(MXU_DTYPE)
    gates = jnp.dot(h, wg_ref[...], preferred_element_type=F32)
    merged = jax.nn.sigmoid(gates[:, 0:d]) * _mm(amix_ref[...], woa_ref[...])
    merged = merged + jax.nn.sigmoid(gates[:, d:2 * d]) * _mm(ob_ref[...], wob_ref[...])
    merged = merged + jax.nn.sigmoid(gates[:, 2 * d:3 * d]) * _mm(oc_ref[...], woc_ref[...])
    x1_ref[...] = x + _mod(ga_ref) * _mm(merged, wout_ref[...])


def _merge(x, amix, ob, oc, mod, mod_spec, g1, woa, wob, woc, wg, wout):
    n, d = x.shape
    tm = min(256, n)
    full = lambda a: pl.BlockSpec(a.shape, lambda i: (0,) * a.ndim)
    tok = lambda a: pl.BlockSpec((tm, a.shape[1]), lambda i: (i, 0))
    return pl.pallas_call(
        _merge_kernel,
        grid=(n // tm,),
        in_specs=[tok(x), tok(amix), tok(ob), tok(oc), mod_spec(tm, 0), mod_spec(tm, 1), mod_spec(tm, 2),
                  full(g1), full(woa), full(wob), full(woc), full(wg), full(wout)],
        out_specs=tok(x),
        out_shape=jax.ShapeDtypeStruct((n, d), F32),
        compiler_params=_params(("parallel",)),
        name="merge_out_proj",
    )(x, amix, ob, oc, mod, mod, mod, g1, woa, wob, woc, wg, wout)


_CAND_ROWS = PEER_TOPK + 7 * 8 + 8


def _tab_pack():
    return 4 // jnp.dtype(TAB_DTYPE).itemsize


def _tab_container():
    return F32 if _tab_pack() == 1 else jnp.uint32


def _store_packed(ref, h, lanes, x):
    pack = _tab_pack()
    rows = SUBLANES * pack
    for kb in range(x.shape[0] // rows):
        tile = x[kb * rows:(kb + 1) * rows].astype(TAB_DTYPE)
        ref[h, kb * SUBLANES:(kb + 1) * SUBLANES, lanes] = tile if pack == 1 else pltpu.bitcast(tile, ref.dtype)


def _load_packed(ref, h, kb, lanes):
    tile = ref[h, kb * SUBLANES:(kb + 1) * SUBLANES, lanes]
    return tile if _tab_pack() == 1 else pltpu.bitcast(tile, TAB_DTYPE)


def _peer_select_kernel(x_ref, sh_ref, sc_ref, g2_ref, wpq_hi_ref, wpq_lo_ref, k1_ref, k2_ref,
                        h2_ref, rk2_ref, p2_ref, n1_ref, e1_ref, sc_scr, rk_scr, vl_scr):
    tm = x_ref.shape[0]
    nh = tm // LANES
    lnh = int(math.log2(nh))
    h2 = _rms_mod(x_ref[...], g2_ref[...], _mod(sc_ref), _mod(sh_ref))
    h2_hi = h2.astype(MXU_DTYPE)
    h2_ref[...] = h2_hi
    h2_lo = (h2 - h2_hi.astype(F32)).astype(MXU_DTYPE)
    w_hi = wpq_hi_ref[...]
    q = (jnp.dot(h2_hi, w_hi, preferred_element_type=F32) + jnp.dot(h2_lo, w_hi, preferred_element_type=F32)
         + jnp.dot(h2_hi, wpq_lo_ref[...], preferred_element_type=F32))
    q_hi = q.astype(MXU_DTYPE)
    q_lo = (q - q_hi.astype(F32)).astype(MXU_DTYPE)
    nt = lambda a, b: lax.dot_general(a, b, (((1,), (1,)), ((), ())), preferred_element_type=F32)
    for h in range(PEER_HEADS):
        cols = slice(h * LANES, (h + 1) * LANES)
        for side, kref in enumerate((k1_ref, k2_ref)):
            k_hi, k_lo = kref[0, h], kref[1, h]
            st = nt(k_hi, q_hi[:, cols]) + nt(k_lo, q_hi[:, cols]) + nt(k_hi, q_lo[:, cols])
            for hf in range(nh):
                sc_scr[2 * h + side, hf] = st[:, hf * LANES:(hf + 1) * LANES]

    key_id = _iota((PEER_KEYS, LANES), 0)
    slot_id = _iota((PEER_TOPK, LANES), 0)

    def topk_fast(idx, bad):
        h, hf = idx >> lnh, idx & (nh - 1)
        s_a, s_b = sc_scr[2 * h, hf], sc_scr[2 * h + 1, hf]

        def it(kk, carry):
            out = []
            for s, vals in carry:
                m = jnp.max(s, axis=0, keepdims=True)
                out.append((jnp.where(s == m, -jnp.inf, s), jnp.where(slot_id == kk, m, vals)))
            return tuple(out)

        zeros = jnp.zeros((PEER_TOPK, LANES), F32)
        (_, v_a), (_, v_b) = lax.fori_loop(0, PEER_TOPK, it, ((s_a, zeros), (s_b, zeros)))
        for side, (s, vals) in enumerate(((s_a, v_a), (s_b, v_b))):
            rank = jnp.zeros((PEER_KEYS, LANES), jnp.int32)
            for kk in range(PEER_TOPK):
                rank = jnp.where(vals[kk:kk + 1] > s, kk + 1, rank)
            ranked = jnp.sum(jnp.where(rank < PEER_TOPK, 1.0, 0.0), axis=0, keepdims=True)
            bad = jnp.maximum(bad, jnp.where(ranked == PEER_TOPK, 0.0, 1.0))
            rk_scr[2 * h + side, hf] = rank
            vl_scr[2 * h + side, hf] = vals
        return bad

    bad = lax.fori_loop(0, PEER_HEADS * nh, topk_fast, jnp.zeros((1, LANES), F32))

    def topk_exact(idx, _):
        i, hf = idx >> lnh, idx & (nh - 1)

        def it(kk, carry):
            s, rank, vals = carry
            m = jnp.max(s, axis=0, keepdims=True)
            first = jnp.min(jnp.where(s == m, key_id, PEER_KEYS), axis=0, keepdims=True)
            hit = key_id == first
            return (jnp.where(hit, -jnp.inf, s), jnp.where(hit, kk, rank), jnp.where(slot_id == kk, m, vals))

        _, rank, vals = lax.fori_loop(
            0, PEER_TOPK, it,
            (sc_scr[i, hf], jnp.full((PEER_KEYS, LANES), PEER_TOPK, jnp.int32), jnp.zeros((PEER_TOPK, LANES), F32)))
        rk_scr[i, hf] = rank
        vl_scr[i, hf] = vals
        return 0

    @pl.when(jnp.max(bad) > 0.0)
    def _():
        lax.fori_loop(0, 2 * PEER_HEADS * nh, topk_exact, 0)

    r8 = _iota((8, LANES), 0)
    r16 = _iota((PEER_TOPK, LANES), 0)

    def candidates(h, hf):
        v1, v2 = vl_scr[2 * h, hf], vl_scr[2 * h + 1, hf]
        e1s, e2s = jnp.exp(v1 - v1[0:1]), jnp.exp(v2 - v2[0:1])
        blocks = [(v1[0:1] + v2, r16, e1s[0:1] * e2s)]
        for i in range(1, 8):
            ok = r8 < PEER_TOPK // (i + 1)
            blocks.append((jnp.where(ok, v1[i:i + 1] + v2[0:8], -jnp.inf), i * PEER_TOPK + r8,
                           e1s[i:i + 1] * e2s[0:8]))
        blocks.append((v1[8:16] + v2[0:1], (r8 + 8) * PEER_TOPK, e1s[8:16] * e2s[0:1]))
        return tuple(jnp.concatenate([b[k] for b in blocks], axis=0) for k in range(3))

    def final_topk(cands, flat):
        def rounds(exact):
            def it(_, carry):
                out = []
                for cnd, taken in carry:
                    m = jnp.max(cnd, axis=0, keepdims=True)
                    hit = cnd == m
                    if exact:
                        first = jnp.min(jnp.where(hit, flat, PEER_TOPK * PEER_TOPK), axis=0, keepdims=True)
                        hit = flat == first
                    out.append((jnp.where(hit, -jnp.inf, cnd), jnp.where(hit, 1.0, taken)))
                return tuple(out)

            init = tuple((c, jnp.zeros((_CAND_ROWS, LANES), F32)) for c in cands)
            return tuple(taken for _, taken in lax.fori_loop(0, PEER_TOPK, it, init))

        fast = rounds(False)
        tied = jnp.zeros((1, LANES), F32)
        for taken in fast:
            tied = jnp.maximum(tied, jnp.where(jnp.sum(taken, axis=0, keepdims=True) == PEER_TOPK, 0.0, 1.0))
        return lax.cond(jnp.max(tied) > 0.0, lambda: rounds(True), lambda: fast)

    def tables(h, hf, taken, num):
        v1, v2 = vl_scr[2 * h, hf], vl_scr[2 * h + 1, hf]
        rank1, rank2 = rk_scr[2 * h, hf], rk_scr[2 * h + 1, hf]
        z = jnp.sum(taken * num, axis=0, keepdims=True)
        counts = [jnp.sum(taken[0:16], axis=0, keepdims=True)]
        for i in range(1, 8):
            counts.append(jnp.sum(taken[8 + 8 * i:16 + 8 * i], axis=0, keepdims=True))
        tail = taken[72:80]
        n_a = jnp.zeros((PEER_KEYS, LANES), F32)
        for i in range(PEER_TOPK):
            n_i = counts[i] if i < 8 else tail[i - 8:i - 7]
            n_a = jnp.where(rank1 == i, n_i, n_a)
        lanes = slice(hf * LANES, (hf + 1) * LANES)
        s1, s2 = sc_scr[2 * h, hf], sc_scr[2 * h + 1, hf]
        n1_ref[h, :, lanes] = n_a
        e1_ref[h, :, lanes] = jnp.where(rank1 < PEER_TOPK, jnp.exp(s1 - v1[0:1]), 0.0)
        _store_packed(rk2_ref, h, lanes, rank2.astype(F32))
        _store_packed(p2_ref, h, lanes, jnp.where(rank2 < PEER_TOPK, jnp.exp(s2 - v2[0:1]), 0.0) / z)

    def per_head(h, _):
        cands = [candidates(h, hf) for hf in range(nh)]
        taken = final_topk([c[0] for c in cands], cands[0][1])
        for hf in range(nh):
            tables(h, hf, taken[hf], cands[hf][2])
        return 0

    lax.fori_loop(0, PEER_HEADS, per_head, 0)


def _peer_select(x1, mod, mod_spec, g2, wpq_hi, wpq_lo, k1pad, k2pad):
    n, d = x1.shape
    tm = min(256, n)
    nh = tm // LANES
    full = lambda a: pl.BlockSpec(a.shape, lambda i: (0,) * a.ndim)
    packed_rows = PEER_KEYS // _tab_pack()
    tab = lambda rows: pl.BlockSpec((PEER_HEADS, rows, tm), lambda i: (0, 0, i))
    tab_shape = lambda rows, dt: jax.ShapeDtypeStruct((PEER_HEADS, rows, n), dt)
    return pl.pallas_call(
        _peer_select_kernel,
        grid=(n // tm,),
        in_specs=[pl.BlockSpec((tm, d), lambda i: (i, 0)), mod_spec(tm, 3), mod_spec(tm, 4), full(g2),
                  full(wpq_hi), full(wpq_lo),
                  full(k1pad), full(k2pad)],
        out_specs=[pl.BlockSpec((tm, d), lambda i: (i, 0)), tab(packed_rows), tab(packed_rows), tab(PEER_KEYS),
                   tab(PEER_KEYS)],
        out_shape=[jax.ShapeDtypeStruct((n, d), MXU_DTYPE), tab_shape(packed_rows, _tab_container()),
                   tab_shape(packed_rows, _tab_container()), tab_shape(PEER_KEYS, F32), tab_shape(PEER_KEYS, F32)],
        scratch_shapes=[pltpu.VMEM((2 * PEER_HEADS, nh, PEER_KEYS, LANES), F32),
                        pltpu.VMEM((2 * PEER_HEADS, nh, PEER_KEYS, LANES), jnp.int32),
                        pltpu.VMEM((2 * PEER_HEADS, nh, PEER_TOPK, LANES), F32)],
        compiler_params=_params(("parallel",)),
        name="peer_select",
    )(x1, mod, mod, g2, wpq_hi, wpq_lo, k1pad, k2pad)


def _gelu_tanh(x):
    c = math.sqrt(2.0 / math.pi)
    half = 0.5 * x
    return half + half * jnp.tanh(x * (c + (0.044715 * c) * (x * x)))


def _peer_dense_kernel(h2_ref, rk2_ref, p2_ref, n1_ref, e1_ref, u0_ref, un_ref, vt_ref, x1_ref, ga_ref, o_ref,
                       acc_ref, sta_ref, stb_ref, wt_ref):
    j = pl.program_id(1)
    te, tt = sta_ref.shape

    def scores_t(u_ref, rows):
        return lax.dot_general(u_ref[rows, :], h2_ref[...], (((1,), (1,)), ((), ())), preferred_element_type=F32)

    @pl.when(j == 0)
    def _():
        acc_ref[...] = jnp.zeros_like(acc_ref)
        sta_ref[...] = scores_t(u0_ref, slice(None))

    a8 = pl.ds(pl.multiple_of(j * SUBLANES, SUBLANES), SUBLANES)
    tab = TAB_DTYPE
    kb_rows = SUBLANES * _tab_pack()
    n_kb = PEER_KEYS // kb_rows
    r_group = SUBLANES // 2
    n_tc = tt // LANES

    def sweep(st_ref, nxt_ref):
        nxt_ref[...] = scores_t(un_ref, slice(None))
        for r0 in range(0, SUBLANES, r_group):
            for tc in range(n_tc):
                lanes = slice(tc * LANES, (tc + 1) * LANES)
                coef = [[None] * r_group for _ in range(n_kb)]
                for h in range(PEER_HEADS):
                    n8, e8 = n1_ref[h, a8, lanes], e1_ref[h, a8, lanes]
                    bcast = lambda t, r: jnp.broadcast_to(t[r:r + 1], (kb_rows, LANES)).astype(tab)
                    nb = [bcast(n8, r0 + ri) for ri in range(r_group)]
                    eb = [bcast(e8, r0 + ri) for ri in range(r_group)]
                    for kb in range(n_kb):
                        rk, p = _load_packed(rk2_ref, h, kb, lanes), _load_packed(p2_ref, h, kb, lanes)
                        for ri in range(r_group):
                            term = jnp.where(rk < nb[ri], p, jnp.zeros_like(p)) * eb[ri]
                            coef[kb][ri] = term if h == 0 else coef[kb][ri] + term
                for kb in range(n_kb):
                    for ri in range(r_group):
                        first = (r0 + ri) * PEER_KEYS + kb * kb_rows
                        rows = slice(first, first + kb_rows)
                        act = _gelu_tanh(st_ref[rows, lanes]).astype(tab)
                        wt_ref[rows, lanes] = (coef[kb][ri] * act).astype(wt_ref.dtype)
            done =slice(r0 * PEER_KEYS, (r0 + r_group) * PEER_KEYS)
            acc_ref[...] += jnp.dot(vt_ref[:, done], wt_ref[done, :], preferred_element_type=F32)

    @pl.when((j & 1) == 0)
    def _():
        sweep(sta_ref, stb_ref)

    @pl.when((j & 1) == 1)
    def _():
        sweep(stb_ref, sta_ref)

    @pl.when(j == pl.num_programs(1) - 1)
    def _():
        o_ref[...] = x1_ref[...] + _mod(ga_ref) * acc_ref[...].T


def _peer_dense(h2, tabs, u, vt, x1, mod, mod_spec):
    n, d = x1.shape
    n_exp = u.shape[0]
    tt = min(512, n)
    te = SUBLANES * PEER_KEYS
    tab = lambda a: pl.BlockSpec((PEER_HEADS, a.shape[1], tt), lambda i, j: (0, 0, i))
    tok = lambda a: pl.BlockSpec((tt, a.shape[1]), lambda i, j: (i, 0))
    ms = mod_spec(tt, 5)
    ga_spec = pl.BlockSpec(ms.block_shape, lambda i, j, f=ms.index_map: f(i))
    ne = n_exp // te
    return pl.pallas_call(
        _peer_dense_kernel,
        grid=(n // tt, ne),
        in_specs=[tok(h2)] + [tab(t) for t in tabs]
                 + [pl.BlockSpec((te, d), lambda i, j: (0, 0)),
                    pl.BlockSpec((te, d), lambda i, j: ((j + 1) % ne, 0)),
                    pl.BlockSpec((d, te), lambda i, j: (0, j)),
                    tok(x1), ga_spec],
        out_specs=tok(x1),
        out_shape=jax.ShapeDtypeStruct((n, d), F32),
        scratch_shapes=[pltpu.VMEM((d, tt), F32), pltpu.VMEM((te, tt), F32), pltpu.VMEM((te, tt), F32),
                        pltpu.VMEM((te, tt), MXU_DTYPE)],
        compiler_params=_params(("parallel", "arbitrary")),
        name="peer_dense",
    )(h2, *tabs, u, u, vt, x1, mod)


def _final_kernel(x_ref, g_ref, o_ref):
    x = x_ref[...]
    o_ref[...] = x * lax.rsqrt(jnp.mean(x * x, axis=-1, keepdims=True) + EPS) * g_ref[...]


def _final_norm(x, g):
    n, d = x.shape
    tm = min(512, n)
    return pl.pallas_call(
        _final_kernel,
        grid=(n // tm,),
        in_specs=[pl.BlockSpec((tm, d), lambda i: (i, 0)), pl.BlockSpec((1, d), lambda i: (0, 0))],
        out_specs=pl.BlockSpec((tm, d), lambda i: (i, 0)),
        out_shape=jax.ShapeDtypeStruct((n, d), F32),
        compiler_params=_params(("parallel",)),
        name="final_norm",
    )(x, g)


def _shared_mod_spec(seq_len, d):
    def spec(tm, chunk):
        return pl.BlockSpec((1, 1, d), lambda i: ((i * tm) // seq_len, 0, chunk))
    return spec


def _per_token_mod_spec(d):
    def spec(tm, chunk):
        return pl.BlockSpec((tm, d), lambda i: (i, chunk))
    return spec


def kernel(x_prompt, x_sample, cache_k, cache_v, state_gla, page_table, c_prompt, c_sample, norm1_g, norm2_g, w_ada, b_ada, w_in, a_norm_g, w_sp, b_sp, w_oa, w_alpha, b_alpha, gla_norm_g, w_ob, lam_q1, lam_k1, lam_q2, lam_k2, diff_norm_g, w_oc, w_out, w_pq, sub_k1, sub_k2, peer_u, peer_v, final_g):
    depth = w_in.shape[0]
    bsz, seq, d = x_prompt.shape
    dbs, dseq, _ = x_sample.shape
    bf = lambda a: a.astype(MXU_DTYPE)

    wa = bf(w_in[:, :, 0:512])
    wb = bf(jnp.concatenate([w_in[:, :, 512:1280], jnp.pad(w_in[:, :, 1280:1296], ((0, 0), (0, 0), (0, 112)))], axis=2))
    wc = bf(w_in[:, :, 1296:2832])
    wg = bf(w_in[:, :, 2832:5904])
    wal = bf(jnp.pad(w_alpha, ((0, 0), (0, 112), (0, 0))))
    woa, wob, woc, wout = bf(w_oa), bf(w_ob), bf(w_oc), bf(w_out)
    wpq_hi = bf(w_pq)
    wpq_lo = bf(w_pq - wpq_hi.astype(F32))
    u_b = bf(peer_u)
    vt_b = bf(jnp.transpose(peer_v, (0, 2, 1)))
    half = sub_k1.shape[-1]
    hi_lo = lambda a: jnp.stack([bf(a), bf(a - bf(a).astype(F32))], axis=1)
    k1pad = hi_lo(jnp.pad(sub_k1, ((0, 0), (0, 0), (0, 0), (0, half))))
    k2pad = hi_lo(jnp.pad(sub_k2, ((0, 0), (0, 0), (0, 0), (half, 0))))
    lamp = jnp.pad(jnp.stack([lam_q1, lam_k1, lam_q2, lam_k2], axis=1), ((0, 0), (0, 0), (0, LANES - lam_q1.shape[-1])))
    gla_g = jnp.tile(gla_norm_g, (1, GLA_HEADS)).reshape(depth, 1, GLA_HEADS * GLA_DV)

    mod_all = _ada(jnp.concatenate([c_prompt, c_sample], axis=0), w_ada, b_ada)

    groups = []
    for name, x, seq_len in (("prompt", x_prompt, seq), ("sample", x_sample, dseq)):
        c = CHUNK if seq_len >= CHUNK else seq_len
        groups.append(dict(name=name, x=x.reshape(-1, d), seq_len=seq_len, c=c))

    outs = {g["name"]: dict(k=[], v=[], s=[], a=[]) for g in groups}
    for l in range(depth):
        lam_init = 0.8 - 0.6 * math.exp(-0.3 * l)
        row = lambda a: a[l].reshape(1, -1)
        for g in groups:
            n, seq_len, c = g["x"].shape[0], g["seq_len"], g["c"]
            if g["name"] == "prompt":
                mod = mod_all[l, :bsz].reshape(bsz, 1, 6 * d)
                mod_spec = _shared_mod_spec(seq_len, d)
                s0 = jnp.zeros((bsz, GLA_HEADS * GLA_DV, GLA_HEADS * GLA_DK), F32)
            else:
                mod = jnp.repeat(mod_all[l, bsz:], seq_len, axis=0)
                mod_spec = _per_token_mod_spec(d)
                s0 = _state_to_bd(state_gla[l])
            wsp_tile = jnp.tile(w_sp[l][:, :c, :c], (1, CHUNK // c, CHUNK // c))
            bsp_full = jnp.repeat(jnp.tile(b_sp[l][:, :c], (1, CHUNK // c)).T, A_WIDTH // A_GROUPS, axis=1)

            amix, avn, bq, bk, bv, br, la, cq, ck, cv = _in_proj(
                g["x"], mod, mod_spec, c, row(norm1_g), wa[l], wb[l], wc[l], wal[l], row(b_alpha), row(a_norm_g),
                wsp_tile, bsp_full)
            ob, s_new = _gla(bq, bk, bv, br, la, s0, gla_g[l], seq_len)
            if g["name"] == "prompt":
                oc = _attn_prompt(cq, ck, cv, lamp[l], row(diff_norm_g), seq_len, lam_init)
            else:
                oc = _attn_sample(cq, ck, cv, cache_k, cache_v, page_table, l, lamp[l], row(diff_norm_g), seq_len,
                                  lam_init)
            x1 = _merge(g["x"], amix, ob, oc, mod, mod_spec, row(norm1_g), woa[l], wob[l], woc[l], wg[l], wout[l])
            h2, *tabs = _peer_select(x1, mod, mod_spec, row(norm2_g), wpq_hi[l], wpq_lo[l], k1pad[l], k2pad[l])
            g["x"] = _peer_dense(h2, tabs, u_b[l], vt_b[l], x1, mod, mod_spec)

            o = outs[g["name"]]
            o["k"].append(ck)
            o["v"].append(cv)
            o["s"].append(_state_from_bd(s_new))
            o["a"].append(avn)

    fg = final_g.reshape(1, d)
    y_prompt = _final_norm(groups[0]["x"], fg).reshape(bsz, seq, d)
    y_sample = _final_norm(groups[1]["x"], fg).reshape(dbs, dseq, d)
    kv_shape = lambda b, t: (depth, b, t, DIFF_HEADS, 2 * DIFF_D)
    op, os_ = outs["prompt"], outs["sample"]
    return (y_prompt, y_sample,
            jnp.stack(op["k"]).reshape(kv_shape(bsz, seq)), jnp.stack(op["v"]).reshape(kv_shape(bsz, seq)),
            jnp.stack(op["s"]),
            jnp.stack(os_["k"]).reshape(kv_shape(dbs, dseq)), jnp.stack(os_["v"]).reshape(kv_shape(dbs, dseq)),
            jnp.stack(os_["s"]),
            jnp.stack(os_["a"]).reshape(depth, dbs, dseq, A_WIDTH))
```
